```python
import jax, jax.numpy as jnp
from jax import lax
import numpy as np

D_MODEL = 1024
BATCH = 8
SEQ = 4096
DEPTH = 1

N_HEADS = 8
N_KV = 2
HEAD_DIM = 64
GQA_R = N_HEADS // N_KV
NSA_WIDTH = N_HEADS * HEAD_DIM
KV_WIDTH = N_KV * HEAD_DIM
CMP_BLOCK = 32
CMP_STRIDE = 16
SEL_BLOCK = 64
SEL_TOPK = 16
N_FORCED_LOCAL = 2
WINDOW = 512
QBLK = 128
FORCE_SCORE = 1e9
POOL_SIZES = (2, 4, 8, 16)
POOL_GROUP = 128
POOL_WIDTH = POOL_GROUP * len(POOL_SIZES)
D_FF = 4 * D_MODEL
ROPE_THETA = 10000.0
EPS = 1e-6
N_MOD = 6
IN_SIZES = (NSA_WIDTH, KV_WIDTH, KV_WIDTH, KV_WIDTH, KV_WIDTH, KV_WIDTH, KV_WIDTH,
            3 * N_HEADS, POOL_WIDTH, 2 * D_MODEL)
IN_WIDTH = int(sum(IN_SIZES))
IN_SPLITS = [int(v) for v in np.cumsum(IN_SIZES)[:-1]]

kernel_name = "hybrid_nsa_pool_block"


def rms_norm(x, g):
    xf = x.astype(jnp.float32)
    y = xf * lax.rsqrt(jnp.mean(xf * xf, axis=-1, keepdims=True) + EPS)
    return (y * g.astype(jnp.float32)).astype(x.dtype)


def rope(x):
    s = x.shape[1]
    half = HEAD_DIM // 2
    inv = ROPE_THETA ** (-jnp.arange(half, dtype=jnp.float32) / half)
    ang = jnp.arange(s, dtype=jnp.float32)[:, None] * inv[None, :]
    cos = jnp.cos(ang)[None, :, None, :]
    sin = jnp.sin(ang)[None, :, None, :]
    xf = x.astype(jnp.float32)
    x1, x2 = xf[..., :half], xf[..., half:]
    return jnp.concatenate([x1 * cos - x2 * sin, x2 * cos + x1 * sin], axis=-1).astype(x.dtype)


def masked_softmax(s, mask):
    s = jnp.where(mask, s.astype(jnp.float32), -jnp.inf)
    m = jnp.max(s, axis=-1, keepdims=True)
    m = jnp.where(jnp.isfinite(m), m, 0.0)
    e = jnp.exp(s - m)
    return e / jnp.maximum(jnp.sum(e, axis=-1, keepdims=True), 1e-30)


def compress(k, pe, w1, w2):
    b, s, g, dh = k.shape
    n_cmp = (s - CMP_BLOCK) // CMP_STRIDE + 1
    idx = np.arange(n_cmp)[:, None] * CMP_STRIDE + np.arange(CMP_BLOCK)[None, :]
    blk = k[:, idx] + pe[None, None, :, None, :]
    blk = blk.transpose(0, 1, 3, 2, 4).reshape(b, n_cmp, g, CMP_BLOCK * dh)
    return jax.nn.gelu(blk @ w1) @ w2


def nsa_attention(q, kc, vc, ks, vs, kw, vw, gate_logits):
    b, s = q.shape[0], q.shape[1]
    nq = s // QBLK
    n_cmp = kc.shape[1]
    n_sel = s // SEL_BLOCK
    topk = min(SEL_TOPK, n_sel)
    scale = HEAD_DIM ** -0.5
    cmp_start = np.arange(n_cmp) * CMP_STRIDE
    cmp_end = jnp.asarray(cmp_start + CMP_BLOCK - 1)
    sel_start = np.arange(n_sel) * SEL_BLOCK
    overlap = jnp.asarray(((cmp_start[:, None] < sel_start[None, :] + SEL_BLOCK) &
                           (cmp_start[:, None] + CMP_BLOCK > sel_start[None, :])).astype(np.float32))
    ks_blk = ks.reshape(b, n_sel, SEL_BLOCK, N_KV, HEAD_DIM).transpose(0, 3, 1, 2, 4)
    vs_blk = vs.reshape(b, n_sel, SEL_BLOCK, N_KV, HEAD_DIM).transpose(0, 3, 1, 2, 4)
    pad = ((0, 0), (WINDOW, 0), (0, 0), (0, 0))
    kw_pad = jnp.pad(kw, pad)
    vw_pad = jnp.pad(vw, pad)
    q_blocks = q.reshape(b, nq, QBLK, N_KV, GQA_R, HEAD_DIM).transpose(1, 0, 2, 3, 4, 5)
    g_blocks = gate_logits.reshape(b, nq, QBLK, N_HEADS, 3).transpose(1, 0, 2, 3, 4)
    bi = jnp.arange(b)[:, None, None, None]
    gi = jnp.arange(N_KV)[None, :, None, None]
    blk_id = jnp.arange(n_sel)

    def one_block(args):
        qb, gb, c = args
        t = c * QBLK + jnp.arange(QBLK)
        sc = jnp.einsum('bqgrd,bngd->bgrqn', qb, kc) * scale
        p_cmp = masked_softmax(sc, cmp_end[None, :] <= t[:, None])
        o_cmp = jnp.einsum('bgrqn,bngd->bqgrd', p_cmp.astype(vc.dtype), vc)
        imp = jnp.einsum('bgrqn,nm->bgqm', p_cmp, overlap)
        cur = t // SEL_BLOCK
        causal_blk = blk_id[None, :] <= cur[:, None]
        forced = (blk_id[None, :] == 0) | (causal_blk & (blk_id[None, :] > cur[:, None] - N_FORCED_LOCAL))
        imp = jnp.where(forced, FORCE_SCORE, jnp.where(causal_blk, imp, -jnp.inf))
        _, idx = lax.top_k(imp, topk)
        kg = ks_blk[bi, gi, idx].reshape(b, N_KV, QBLK, topk * SEL_BLOCK, HEAD_DIM)
        vg = vs_blk[bi, gi, idx].reshape(b, N_KV, QBLK, topk * SEL_BLOCK, HEAD_DIM)
        kpos = (idx[..., None] * SEL_BLOCK + jnp.arange(SEL_BLOCK)).reshape(b, N_KV, QBLK, topk * SEL_BLOCK)
        sel_mask = (kpos <= t[None, None, :, None])[:, :, None]
        ss = jnp.einsum('bqgrd,bgqjd->bgrqj', qb, kg) * scale
        p_sel = masked_softmax(ss, sel_mask)
        o_sel = jnp.einsum('bgrqj,bgqjd->bqgrd', p_sel.astype(vg.dtype), vg)
        kwb = lax.dynamic_slice_in_dim(kw_pad, c * QBLK, QBLK + WINDOW, axis=1)
        vwb = lax.dynamic_slice_in_dim(vw_pad, c * QBLK, QBLK + WINDOW, axis=1)
        kp = c * QBLK - WINDOW + jnp.arange(QBLK + WINDOW)
        wmask = (kp[None, :] <= t[:, None]) & (kp[None, :] > t[:, None] - WINDOW) & (kp[None, :] >= 0)
        sw = jnp.einsum('bqgrd,bjgd->bgrqj', qb, kwb) * scale
        p_win = masked_softmax(sw, wmask)
        o_win = jnp.einsum('bgrqj,bjgd->bqgrd', p_win.astype(vwb.dtype), vwb)
        g = jax.nn.sigmoid(gb.astype(jnp.float32)).reshape(b, QBLK, N_KV, GQA_R, 3).astype(qb.dtype)
        return g[..., 0:1] * o_cmp + g[..., 1:2] * o_sel + g[..., 2:3] * o_win

    out = lax.map(one_block, (q_blocks, g_blocks, jnp.arange(nq)))
    return out.transpose(1, 0, 2, 3, 4, 5).reshape(b, s, NSA_WIDTH)


def pool_mixer(u, w_pool, pool_scale):
    b, s, _ = u.shape
    uf = u.astype(jnp.float32)
    cs = jnp.concatenate([jnp.zeros((b, 1, POOL_WIDTH), jnp.float32), jnp.cumsum(uf, axis=1)], axis=1)
    t = np.arange(s)
    outs = []
    for gidx, w in enumerate(POOL_SIZES):
        sl = slice(gidx * POOL_GROUP, (gidx + 1) * POOL_GROUP)
        start = np.maximum(t + 1 - w, 0)
        cnt = jnp.asarray((t + 1 - start).astype(np.float32))[None, :, None]
        mean = (cs[:, t + 1, sl] - cs[:, start, sl]) / cnt
        outs.append(mean - uf[:, :, sl])
    pooled = jnp.stack(outs, axis=2).astype(u.dtype)
    mixed = jnp.einsum('bsgc,gcd->bsgd', pooled, w_pool).reshape(b, s, POOL_WIDTH)
    return mixed * pool_scale


def setup_inputs(seed: int = 0) -> dict:
    key = jax.random.key(seed)
    ks = jax.random.split(key, 24)
    n = lambda k, shape, s: jax.random.normal(k, shape, jnp.float32) * s
    L = DEPTH
    return {
        "x": n(ks[0], (BATCH, SEQ, D_MODEL), 1.0),
        "c": n(ks[1], (BATCH, D_MODEL), 1.0),
        "w_ada": n(ks[2], (L, D_MODEL, N_MOD * D_MODEL), D_MODEL ** -0.5),
        "b_ada": n(ks[3], (L, N_MOD * D_MODEL), 0.02),
        "g_pre_mix": 1.0 + n(ks[4], (L, D_MODEL), 0.05),
        "g_post_mix": 1.0 + n(ks[5], (L, D_MODEL), 0.05),
        "w_in": n(ks[6], (L, D_MODEL, IN_WIDTH), D_MODEL ** -0.5),
        "cmp_pe_k": n(ks[7], (L, CMP_BLOCK, HEAD_DIM), 0.1),
        "cmp_w1_k": n(ks[8], (L, CMP_BLOCK * HEAD_DIM, HEAD_DIM), (CMP_BLOCK * HEAD_DIM) ** -0.5),
        "cmp_w2_k": n(ks[9], (L, HEAD_DIM, HEAD_DIM), HEAD_DIM ** -0.5),
        "cmp_pe_v": n(ks[10], (L, CMP_BLOCK, HEAD_DIM), 0.1),
        "cmp_w1_v": n(ks[11], (L, CMP_BLOCK * HEAD_DIM, HEAD_DIM), (CMP_BLOCK * HEAD_DIM) ** -0.5),
        "cmp_w2_v": n(ks[12], (L, HEAD_DIM, HEAD_DIM), HEAD_DIM ** -0.5),
        "w_pool": n(ks[13], (L, len(POOL_SIZES), POOL_GROUP, POOL_GROUP), POOL_GROUP ** -0.5),
        "pool_scale": 1.0 + n(ks[14], (L, POOL_WIDTH), 0.1),
        "w_br_nsa": n(ks[15], (L, NSA_WIDTH, D_MODEL), NSA_WIDTH ** -0.5),
        "w_br_pool": n(ks[16], (L, POOL_WIDTH, D_MODEL), POOL_WIDTH ** -0.5),
        "w_out": n(ks[17], (L, D_MODEL, D_MODEL), D_MODEL ** -0.5),
        "g_pre_mlp": 1.0 + n(ks[18], (L, D_MODEL), 0.05),
        "g_post_mlp": 1.0 + n(ks[19], (L, D_MODEL), 0.05),
        "w_fc1": n(ks[20], (L, D_MODEL, D_FF), D_MODEL ** -0.5),
        "w_fc2": n(ks[21], (L, D_FF, D_MODEL), D_FF ** -0.5),
    }


def reference(x, c, w_ada, b_ada, g_pre_mix, g_post_mix, w_in, cmp_pe_k, cmp_w1_k, cmp_w2_k,
              cmp_pe_v, cmp_w1_v, cmp_w2_v, w_pool, pool_scale, w_br_nsa, w_br_pool, w_out,
              g_pre_mlp, g_post_mlp, w_fc1, w_fc2):
    b, s, _ = x.shape
    for l in range(DEPTH):
        mod = (jax.nn.silu(c) @ w_ada[l] + b_ada[l]).astype(x.dtype)[:, None, :]
        sh1, sc1, gt1, sh2, sc2, gt2 = jnp.split(mod, N_MOD, axis=-1)

        h = rms_norm(x, g_pre_mix[l]) * (1.0 + sc1) + sh1
        proj = h @ w_in[l]
        q, kc, vc, ksl, vsl, kwn, vwn, nsa_g, u, br_g = jnp.split(proj, IN_SPLITS, axis=-1)
        kv = lambda a: a.reshape(b, s, N_KV, HEAD_DIM)
        q = rope(q.reshape(b, s, N_HEADS, HEAD_DIM))
        kc_c = compress(rope(kv(kc)), cmp_pe_k[l], cmp_w1_k[l], cmp_w2_k[l])
        vc_c = compress(kv(vc), cmp_pe_v[l], cmp_w1_v[l], cmp_w2_v[l])
        o_nsa = nsa_attention(q, kc_c, vc_c, rope(kv(ksl)), kv(vsl), rope(kv(kwn)), kv(vwn),
                              nsa_g.reshape(b, s, N_HEADS, 3))
        o_pool = pool_mixer(u, w_pool[l], pool_scale[l])
        gate_a, gate_b = jnp.split(jax.nn.sigmoid(br_g), 2, axis=-1)
        merged = gate_a * (o_nsa @ w_br_nsa[l]) + gate_b * (o_pool @ w_br_pool[l])
        y = merged @ w_out[l]
        x = x + gt1 * rms_norm(y, g_post_mix[l])

        h = rms_norm(x, g_pre_mlp[l]) * (1.0 + sc2) + sh2
        y = jnp.square(jax.nn.relu(h @ w_fc1[l])) @ w_fc2[l]
        x = x + gt2 * rms_norm(y, g_post_mlp[l])
    return x
```

```python
import functools

import numpy as np
import jax
import jax.numpy as jnp
from jax import lax
from jax.experimental import pallas as pl
from jax.experimental.pallas import tpu as pltpu

F32 = jnp.float32
BF16 = jnp.bfloat16

D_MODEL = 1024
N_HEADS = 8
N_KV = 2
HEAD_DIM = 64
GQA_R = N_HEADS // N_KV
NSA_WIDTH = N_HEADS * HEAD_DIM
KV_WIDTH = N_KV * HEAD_DIM
CMP_BLOCK = 32
CMP_STRIDE = 16
SEL_BLOCK = 64
SEL_TOPK = 16
N_FORCED_LOCAL = 2
WINDOW = 512
FORCE_SCORE = 1e9
POOL_SIZES = (2, 4, 8, 16)
POOL_GROUP = 128
POOL_WIDTH = POOL_GROUP * len(POOL_SIZES)
D_FF = 4 * D_MODEL
ROPE_THETA = 10000.0
EPS = 1e-6
N_MOD = 6
IN_SIZES = (NSA_WIDTH, KV_WIDTH, KV_WIDTH, KV_WIDTH, KV_WIDTH, KV_WIDTH, KV_WIDTH,
            3 * N_HEADS, POOL_WIDTH, 2 * D_MODEL)
IN_SPLITS = [int(v) for v in np.cumsum(IN_SIZES)[:-1]]

LANES = 128
POOL_HALO = 16
BLOCK_MASK_BIAS = -float(2 ** 30)
NOT_CAUSAL_SCORE = -1e30

C_Q = 0
C_KC = C_Q + NSA_WIDTH
C_KS = C_KC + KV_WIDTH
C_KW = C_KS + KV_WIDTH
C_VC = C_KW + KV_WIDTH
C_VS = C_VC + KV_WIDTH
C_VW = C_VS + KV_WIDTH
C_GATE = C_VW + KV_WIDTH
C_U = C_GATE + N_KV * LANES
IN_COLS = C_U + POOL_WIDTH

TM_PROJ = 512
TQ = 128
TK_SEL = 512
FF_CHUNK = 1024
VMEM_LIMIT = 56 * 1024 * 1024


def _cparams(n_axes):
    return pltpu.CompilerParams(dimension_semantics=("arbitrary",) * n_axes,
                                vmem_limit_bytes=VMEM_LIMIT)


def _dot(a, b):
    return jnp.dot(a, b, preferred_element_type=F32)


def _dot_nt(a, b):
    return lax.dot_general(a, b, (((1,), (1,)), ((), ())), preferred_element_type=F32)


def _rms(x):
    return x * lax.rsqrt(jnp.mean(x * x, axis=-1, keepdims=True) + EPS)


def _mod_kernel(c_ref, w_ref, b_ref, o_ref):
    c = c_ref[...]
    a = c * jax.nn.sigmoid(c)
    o_ref[...] = jnp.dot(a, w_ref[...], preferred_element_type=F32,
                         precision=lax.Precision.HIGHEST) + b_ref[...]


def _modulation(c, w_ada, b_ada):
    b, d = c.shape
    n = w_ada.shape[1]
    tn = 1024
    return pl.pallas_call(
        _mod_kernel,
        grid=(n // tn,),
        in_specs=[pl.BlockSpec((b, d), lambda j: (0, 0)),
                  pl.BlockSpec((d, tn), lambda j: (0, j)),
                  pl.BlockSpec((1, tn), lambda j: (0, j))],
        out_specs=pl.BlockSpec((b, tn), lambda j: (0, j)),
        out_shape=jax.ShapeDtypeStruct((b, n), F32),
        compiler_params=_cparams(1),
        name="modulation",
    )(c, w_ada, b_ada.reshape(1, n))


def _inproj_kernel(x_ref, mod_ref, g_ref, w_ref, cos_ref, sin_ref,
                   q_ref, kc_ref, vc_ref, ks_ref, kw_ref, vs_ref, vw_ref, gl_ref, u_ref):
    tm = x_ref.shape[1]
    h = (_rms(x_ref[0]) * g_ref[...]) * (1.0 + mod_ref[0, 1:2, :]) + mod_ref[0, 0:1, :]
    p = _dot(h.astype(BF16), w_ref[...])

    lane = lax.broadcasted_iota(jnp.int32, (tm, LANES), 1)
    lo = lane < HEAD_DIM
    first_half = (lane % HEAD_DIM) < (HEAD_DIM // 2)
    cos = cos_ref[...]
    sin = sin_ref[...]

    def rope(v):
        rot = jnp.where(first_half, pltpu.roll(v, LANES - HEAD_DIM // 2, 1),
                        pltpu.roll(v, HEAD_DIM // 2, 1))
        return v * cos + rot * sin

    def swap(v):
        return pltpu.roll(v, HEAD_DIM, 1)

    def slab(c0):
        return p[:, c0:c0 + LANES]

    for j in range(N_HEADS // 2):
        v = rope(slab(C_Q + j * LANES))
        q_ref[0, 2 * j] = jnp.where(lo, v, 0.0).astype(BF16)
        q_ref[0, 2 * j + 1] = jnp.where(lo, swap(v), 0.0).astype(BF16)

    v = rope(slab(C_KC))
    kc_ref[0, 0] = v[:, :HEAD_DIM]
    kc_ref[0, 1] = v[:, HEAD_DIM:]
    v = slab(C_VC)
    vc_ref[0, 0] = v[:, :HEAD_DIM]
    vc_ref[0, 1] = v[:, HEAD_DIM:]

    pos = pl.program_id(1) * tm + lax.broadcasted_iota(jnp.int32, (tm, LANES), 0)
    onehot = jnp.where(lane - HEAD_DIM == pos // SEL_BLOCK, 1.0, 0.0)
    v = rope(slab(C_KS))
    ks_ref[0, 0] = jnp.where(lo, v, onehot).astype(BF16)
    ks_ref[0, 1] = jnp.where(lo, swap(v), onehot).astype(BF16)
    v = rope(slab(C_KW))
    kw_ref[0, 0] = jnp.where(lo, v, 0.0).astype(BF16)
    kw_ref[0, 1] = jnp.where(lo, swap(v), 0.0).astype(BF16)

    for c0, ref in ((C_VS, vs_ref), (C_VW, vw_ref)):
        v = slab(c0)
        sv = swap(v)
        ref[0, 0, 0] = jnp.where(lo, v, 0.0).astype(BF16)
        ref[0, 0, 1] = jnp.where(lo, 0.0, sv).astype(BF16)
        ref[0, 1, 0] = jnp.where(lo, sv, 0.0).astype(BF16)
        ref[0, 1, 1] = jnp.where(lo, 0.0, v).astype(BF16)

    for g in range(N_KV):
        gl_ref[0, g] = slab(C_GATE + g * LANES)
    u_ref[0] = p[:, C_U:C_U + POOL_WIDTH]


def _inproj(x, mod, g_pre, w_cat, cos_t, sin_t):
    b, s, d = x.shape
    tm = min(TM_PROJ, s)
    nt = s // tm
    const = lambda bi, i: (0, 0)
    out_shape = (
        jax.ShapeDtypeStruct((b, N_HEADS, s, LANES), BF16),
        jax.ShapeDtypeStruct((b, N_KV, s, HEAD_DIM), F32),
        jax.ShapeDtypeStruct((b, N_KV, s, HEAD_DIM), F32),
        jax.ShapeDtypeStruct((b, N_KV, s, LANES), BF16),
        jax.ShapeDtypeStruct((b, N_KV, s, LANES), BF16),
        jax.ShapeDtypeStruct((b, N_KV, 2, s, LANES), BF16),
        jax.ShapeDtypeStruct((b, N_KV, 2, s, LANES), BF16),
        jax.ShapeDtypeStruct((b, N_KV, s, LANES), F32),
        jax.ShapeDtypeStruct((b, s, POOL_WIDTH), F32),
    )
    kv4 = lambda w: pl.BlockSpec((1, N_KV, tm, w), lambda bi, i: (bi, 0, i, 0))
    kv5 = pl.BlockSpec((1, N_KV, 2, tm, LANES), lambda bi, i: (bi, 0, 0, i, 0))
    out_specs = (
        pl.BlockSpec((1, N_HEADS, tm, LANES), lambda bi, i: (bi, 0, i, 0)),
        kv4(HEAD_DIM), kv4(HEAD_DIM), kv4(LANES), kv4(LANES), kv5, kv5, kv4(LANES),
        pl.BlockSpec((1, tm, POOL_WIDTH), lambda bi, i: (bi, i, 0)),
    )
    return pl.pallas_call(
        _inproj_kernel,
        grid=(b, nt),
        in_specs=[pl.BlockSpec((1, tm, d), lambda bi, i: (bi, i, 0)),
                  pl.BlockSpec((1, N_MOD, d), lambda bi, i: (bi, 0, 0)),
                  pl.BlockSpec((1, d), const),
                  pl.BlockSpec((d, IN_COLS), const),
                  pl.BlockSpec((tm, LANES), lambda bi, i: (i, 0)),
                  pl.BlockSpec((tm, LANES), lambda bi, i: (i, 0))],
        out_specs=out_specs,
        out_shape=out_shape,
        compiler_params=_cparams(2),
        name="inproj",
    )(x, mod, g_pre, w_cat, cos_t, sin_t)


def _gelu_tanh(x):
    return 0.5 * x * (1.0 + jnp.tanh(np.sqrt(2.0 / np.pi) * (x + 0.044715 * (x * x * x))))


def _compress_kernel(k_ref, v_ref, pek_ref, pev_ref, w1k_ref, w1v_ref, w2k_ref, w2v_ref,
                     kc_ref, vc_ref):
    nc = k_ref.shape[2]

    def run(a, pe_ref, w1_ref, w2_ref):
        pa = _dot((a + pe_ref[0:1, :]).astype(BF16), w1_ref[0])
        pb = _dot((a + pe_ref[1:2, :]).astype(BF16), w1_ref[1])
        pb = jnp.concatenate([pb[1:], jnp.zeros((1, HEAD_DIM), F32)], axis=0)
        hid = _gelu_tanh(pa + pb)
        return _dot(hid.astype(BF16), w2_ref[...])

    zeros = jnp.zeros((nc, HEAD_DIM), F32)
    kc = run(k_ref[0, 0], pek_ref, w1k_ref, w2k_ref)
    kc_ref[0, 0] = jnp.concatenate([kc, zeros], axis=1).astype(BF16)
    vc = run(v_ref[0, 0], pev_ref, w1v_ref, w2v_ref)
    vc_ref[0, 0, 0] = jnp.concatenate([vc, zeros], axis=1).astype(BF16)
    vc_ref[0, 0, 1] = jnp.concatenate([zeros, vc], axis=1).astype(BF16)


def _compress(kc4, vc4, pek, pev, w1k, w1v, w2k, w2v):
    b, g, nc, width = kc4.shape
    half = CMP_STRIDE * HEAD_DIM
    blk = pl.BlockSpec((1, 1, nc, width), lambda bi, gi: (bi, gi, 0, 0))
    c2 = lambda bi, gi: (0, 0)
    c3 = lambda bi, gi: (0, 0, 0)
    return pl.pallas_call(
        _compress_kernel,
        grid=(b, g),
        in_specs=[blk, blk,
                  pl.BlockSpec((2, half), c2), pl.BlockSpec((2, half), c2),
                  pl.BlockSpec((2, half, HEAD_DIM), c3), pl.BlockSpec((2, half, HEAD_DIM), c3),
                  pl.BlockSpec((HEAD_DIM, HEAD_DIM), c2), pl.BlockSpec((HEAD_DIM, HEAD_DIM), c2)],
        out_specs=(pl.BlockSpec((1, 1, nc, LANES), lambda bi, gi: (bi, gi, 0, 0)),
                   pl.BlockSpec((1, 1, 2, nc, LANES), lambda bi, gi: (bi, gi, 0, 0, 0))),
        out_shape=(jax.ShapeDtypeStruct((b, g, nc, LANES), BF16),
                   jax.ShapeDtypeStruct((b, g, 2, nc, LANES), BF16)),
        compiler_params=_cparams(2),
        name="compress",
    )(kc4, vc4, pek, pev, w1k, w1v, w2k, w2v)


def _pair(lo, a, b):
    return jnp.where(lo, a, b)


def _attn_kernel(q_ref, kc_ref, vc_ref, ks_ref, vs_ref, kw_ref, vw_ref, gl_ref, ovl_ref, o_ref):
    tq = q_ref.shape[2]
    nc = kc_ref.shape[2]
    s_len = ks_ref.shape[2]
    rows = GQA_R * tq
    q0 = pl.program_id(2) * tq

    q4 = q_ref[0]
    t_col = q0 + lax.broadcasted_iota(jnp.int32, (tq, 1), 0)
    t_rows = jnp.concatenate([t_col] * GQA_R, axis=0)
    lane = lax.broadcasted_iota(jnp.int32, (tq, LANES), 1)
    lo = lane < HEAD_DIM

    def pv_pairs(pb, v_lo, v_hi):
        o01 = _dot(pb[0:tq], v_lo) + _dot(pb[tq:2 * tq], v_hi)
        o23 = _dot(pb[2 * tq:3 * tq], v_lo) + _dot(pb[3 * tq:], v_hi)
        return o01, o23

    def pair_cols(col):
        return (_pair(lo, col[0:tq], col[tq:2 * tq]),
                _pair(lo, col[2 * tq:3 * tq], col[3 * tq:]))

    sc = _dot_nt(q4.reshape(rows, LANES), kc_ref[0, 0])
    cmp_end = lax.broadcasted_iota(jnp.int32, (1, nc), 1) * CMP_STRIDE + (CMP_BLOCK - 1)
    sc = jnp.where(cmp_end <= t_rows, sc, -jnp.inf)
    m = jnp.max(sc, axis=-1, keepdims=True)
    m = jnp.where(m == -jnp.inf, 0.0, m)
    e = jnp.exp(sc - m)
    p_cmp = e / jnp.maximum(jnp.sum(e, axis=-1, keepdims=True), 1e-30)
    cmp01, cmp23 = pv_pairs(p_cmp.astype(BF16), vc_ref[0, 0, 0], vc_ref[0, 0, 1])

    p_sum = p_cmp[0:tq] + p_cmp[tq:2 * tq] + p_cmp[2 * tq:3 * tq] + p_cmp[3 * tq:]
    imp = jnp.dot(p_sum, ovl_ref[...], preferred_element_type=F32,
                  precision=lax.Precision.HIGHEST)
    blk = lane - HEAD_DIM
    cur = t_col // SEL_BLOCK
    causal = (blk >= 0) & (blk <= cur)
    forced = (blk == 0) | (causal & (blk > cur - N_FORCED_LOCAL))
    val = jnp.where(forced, FORCE_SCORE, jnp.where(causal, imp, NOT_CAUSAL_SCORE))
    val = jnp.where(blk >= 0, val, -jnp.inf)
    sel = jnp.zeros((tq, LANES), jnp.bool_)
    for _ in range(SEL_TOPK):
        mx = jnp.max(val, axis=-1, keepdims=True)
        first = jnp.min(jnp.where(val == mx, lane, 2 * LANES), axis=-1, keepdims=True)
        hit = lane == first
        sel = sel | hit
        val = jnp.where(hit, -jnp.inf, val)
    bias = jnp.where(sel | (blk < 0), 0.0, BLOCK_MASK_BIAS).astype(BF16)
    q_aug = (q4 + bias[None]).reshape(rows, LANES)

    tk = min(TK_SEL, s_len)
    n_kt = (q0 + tq + tk - 1) // tk

    def sel_step(j, carry):
        m_i, l_i, a01, a23 = carry
        k0 = pl.multiple_of(j * tk, tk)
        s = _dot_nt(q_aug, ks_ref[0, 0, pl.ds(k0, tk), :])
        kpos = k0 + lax.broadcasted_iota(jnp.int32, (1, tk), 1)
        s = jnp.where(kpos <= t_rows, s, -jnp.inf)
        m_new = jnp.maximum(m_i, jnp.max(s, axis=-1, keepdims=True))
        alpha = jnp.exp(m_i - m_new)
        p = jnp.exp(s - m_new)
        l_new = alpha * l_i + jnp.sum(p, axis=-1, keepdims=True)
        o01, o23 = pv_pairs(p.astype(BF16), vs_ref[0, 0, 0, pl.ds(k0, tk), :],
                            vs_ref[0, 0, 1, pl.ds(k0, tk), :])
        al01, al23 = pair_cols(alpha)
        return m_new, l_new, a01 * al01 + o01, a23 * al23 + o23

    zero_acc = jnp.zeros((tq, LANES), F32)
    _, l_sel, sel01, sel23 = lax.fori_loop(
        0, n_kt, sel_step,
        (jnp.full((rows, 1), -jnp.inf, F32), jnp.zeros((rows, 1), F32), zero_acc, zero_acc))

    wlen = min(WINDOW + tq, s_len)
    w0 = pl.multiple_of(jnp.maximum(q0 + tq - wlen, 0), tq)
    sw = _dot_nt(q_aug, kw_ref[0, 0, pl.ds(w0, wlen), :])
    kp = w0 + lax.broadcasted_iota(jnp.int32, (1, wlen), 1)
    sw = jnp.where((kp <= t_rows) & (kp > t_rows - WINDOW), sw, -jnp.inf)
    m_w = jnp.max(sw, axis=-1, keepdims=True)
    e_w = jnp.exp(sw - m_w)
    l_win = jnp.sum(e_w, axis=-1, keepdims=True)
    win01, win23 = pv_pairs(e_w.astype(BF16), vw_ref[0, 0, 0, pl.ds(w0, wlen), :],
                            vw_ref[0, 0, 1, pl.ds(w0, wlen), :])

    sg = jax.nn.sigmoid(gl_ref[0, 0])
    gcol = lambda r, j: sg[:, 3 * r + j:3 * r + j + 1]
    gate = lambda j: jnp.concatenate([gcol(r, j) for r in range(GQA_R)], axis=0)
    gc01, gc23 = pair_cols(gate(0))
    gs01, gs23 = pair_cols(gate(1) / l_sel)
    gw01, gw23 = pair_cols(gate(2) / l_win)
    out01 = gc01 * cmp01 + gs01 * sel01 + gw01 * win01
    out23 = gc23 * cmp23 + gs23 * sel23 + gw23 * win23
    o_ref[0] = jnp.concatenate([out01, out23], axis=1).astype(BF16)


def _attention(q, kc_aug, vc_p, ks_aug, vs_p, kw_aug, vw_p, gl, ovl):
    b, _, s, _ = q.shape
    nc = kc_aug.shape[2]
    tq = min(TQ, s)
    kv = lambda n: pl.BlockSpec((1, 1, n, LANES), lambda bi, gi, i: (bi, gi, 0, 0))
    kv2 = lambda n: pl.BlockSpec((1, 1, 2, n, LANES), lambda bi, gi, i: (bi, gi, 0, 0, 0))
    return pl.pallas_call(
        _attn_kernel,
        grid=(b, N_KV, s // tq),
        in_specs=[pl.BlockSpec((1, GQA_R, tq, LANES), lambda bi, gi, i: (bi, gi, i, 0)),
                  kv(nc), kv2(nc), kv(s), kv2(s), kv(s), kv2(s),
                  pl.BlockSpec((1, 1, tq, LANES), lambda bi, gi, i: (bi, gi, i, 0)),
                  pl.BlockSpec((nc, LANES), lambda bi, gi, i: (0, 0))],
        out_specs=pl.BlockSpec((1, tq, GQA_R * HEAD_DIM), lambda bi, gi, i: (bi, i, gi)),
        out_shape=jax.ShapeDtypeStruct((b, s, NSA_WIDTH), BF16),
        compiler_params=_cparams(3),
        name="nsa_attention",
    )(q, kc_aug, vc_p, ks_aug, vs_p, kw_aug, vw_p, gl, ovl)


def _merge_kernel(x_ref, mod_ref, gpre_ref, gpost_ref, on_ref, u_ref, uprev_ref, wbrg_ref,
                  wpool_ref, pscale_ref, wbn_ref, wbp_ref, wout_ref, o_ref):
    tm = x_ref.shape[1]
    i = pl.program_id(1)
    x = x_ref[0]
    h = (_rms(x) * gpre_ref[...]) * (1.0 + mod_ref[0, 1:2, :]) + mod_ref[0, 0:1, :]
    br = jax.nn.sigmoid(_dot(h.astype(BF16), wbrg_ref[...]))

    u = u_ref[0]
    prev = jnp.where(i > 0, uprev_ref[0], 0.0)
    ext = jnp.concatenate([prev, u], axis=0)
    tpos = i * tm + lax.broadcasted_iota(jnp.int32, (tm, 1), 0)
    mixed = []
    for gidx, w in enumerate(POOL_SIZES):
        acc = ext[:, gidx * POOL_GROUP:(gidx + 1) * POOL_GROUP]
        span = 1
        while span < w:
            acc = acc + pltpu.roll(acc, span, 0)
            span *= 2
        cnt = jnp.minimum(tpos + 1, w).astype(F32)
        pooled = acc[POOL_HALO:] / cnt - u[:, gidx * POOL_GROUP:(gidx + 1) * POOL_GROUP]
        mixed.append(_dot(pooled.astype(BF16), wpool_ref[gidx]))
    o_pool = jnp.concatenate(mixed, axis=1) * pscale_ref[...]

    merged = (br[:, :D_MODEL] * _dot(on_ref[0], wbn_ref[...])
              + br[:, D_MODEL:] * _dot(o_pool.astype(BF16), wbp_ref[...]))
    y = _dot(merged.astype(BF16), wout_ref[...])
    o_ref[0] = x + mod_ref[0, 2:3, :] * (_rms(y) * gpost_ref[...])


def _merge(x, mod, g_pre, g_post, o_nsa, u, w_brg, w_pool, pool_scale, w_br_nsa, w_br_pool, w_out):
    b, s, d = x.shape
    tm = min(TM_PROJ, s)
    halo_per_tile = tm // POOL_HALO
    tile = lambda w: pl.BlockSpec((1, tm, w), lambda bi, i: (bi, i, 0))
    c2 = lambda shape: pl.BlockSpec(shape, lambda bi, i: (0, 0))
    return pl.pallas_call(
        _merge_kernel,
        grid=(b, s // tm),
        in_specs=[tile(d),
                  pl.BlockSpec((1, N_MOD, d), lambda bi, i: (bi, 0, 0)),
                  c2((1, d)), c2((1, d)),
                  tile(NSA_WIDTH), tile(POOL_WIDTH),
                  pl.BlockSpec((1, POOL_HALO, POOL_WIDTH),
                               lambda bi, i: (bi, jnp.maximum(i * halo_per_tile - 1, 0), 0)),
                  c2((d, 2 * d)),
                  pl.BlockSpec((len(POOL_SIZES), POOL_GROUP, POOL_GROUP), lambda bi, i: (0, 0, 0)),
                  c2((1, POOL_WIDTH)),
                  c2((NSA_WIDTH, d)), c2((POOL_WIDTH, d)), c2((d, d))],
        out_specs=tile(d),
        out_shape=jax.ShapeDtypeStruct((b, s, d), F32),
        compiler_params=_cparams(2),
        name="merge",
    )(x, mod, g_pre, g_post, o_nsa, u, u, w_brg, w_pool, pool_scale, w_br_nsa, w_br_pool, w_out)


def _mlp_kernel(x_ref, mod_ref, gpre_ref, gpost_ref, w1_ref, w2_ref, o_ref):
    x = x_ref[0]
    h = ((_rms(x) * gpre_ref[...]) * (1.0 + mod_ref[0, 4:5, :]) + mod_ref[0, 3:4, :]).astype(BF16)
    y = jnp.zeros(x.shape, F32)
    for c0 in range(0, D_FF, FF_CHUNK):
        a = jnp.maximum(_dot(h, w1_ref[:, c0:c0 + FF_CHUNK]), 0.0)
        y = y + _dot((a * a).astype(BF16), w2_ref[c0:c0 + FF_CHUNK, :])
    o_ref[0] = x + mod_ref[0, 5:6, :] * (_rms(y) * gpost_ref[...])


def _mlp(x, mod, g_pre, g_post, w_fc1, w_fc2):
    b, s, d = x.shape
    tm = min(TM_PROJ, s)
    tile = pl.BlockSpec((1, tm, d), lambda bi, i: (bi, i, 0))
    resident = lambda shape: pl.BlockSpec(shape, lambda bi, i: (0, 0), pipeline_mode=pl.Buffered(1))
    return pl.pallas_call(
        _mlp_kernel,
        grid=(b, s // tm),
        in_specs=[tile,
                  pl.BlockSpec((1, N_MOD, d), lambda bi, i: (bi, 0, 0)),
                  pl.BlockSpec((1, d), lambda bi, i: (0, 0)),
                  pl.BlockSpec((1, d), lambda bi, i: (0, 0)),
                  resident((d, D_FF)), resident((D_FF, d))],
        out_specs=tile,
        out_shape=jax.ShapeDtypeStruct((b, s, d), F32),
        compiler_params=_cparams(2),
        name="mlp",
    )(x, mod, g_pre, g_post, w_fc1, w_fc2)


def _rope_tables(s):
    half = HEAD_DIM // 2
    inv = ROPE_THETA ** (-np.arange(half, dtype=np.float32) / half)
    ang = np.arange(s, dtype=np.float32)[:, None] * inv[None, :].astype(np.float32)
    cos = np.cos(ang).astype(np.float32)
    sin = np.sin(ang).astype(np.float32)
    reps = LANES // HEAD_DIM
    cos_t = np.tile(np.concatenate([cos, cos], axis=1), (1, reps))
    sin_t = np.tile(np.concatenate([-sin, sin], axis=1), (1, reps))
    return jnp.asarray(cos_t), jnp.asarray(sin_t)


def _overlap_table(nc, n_sel):
    cmp_start = np.arange(nc) * CMP_STRIDE
    sel_start = np.arange(n_sel) * SEL_BLOCK
    ovl = ((cmp_start[:, None] < sel_start[None, :] + SEL_BLOCK)
           & (cmp_start[:, None] + CMP_BLOCK > sel_start[None, :])).astype(np.float32)
    out = np.zeros((nc, LANES), np.float32)
    out[:, HEAD_DIM:HEAD_DIM + n_sel] = ovl
    return jnp.asarray(out)


def _pack_in_weights(w_in):
    q, kc, vc, ks, vs, kw, vw, gate, u, brg = jnp.split(w_in, IN_SPLITS, axis=-1)
    d = w_in.shape[0]
    gate = gate.reshape(d, N_KV, GQA_R * 3)
    gate = jnp.pad(gate, ((0, 0), (0, 0), (0, LANES - GQA_R * 3))).reshape(d, N_KV * LANES)
    cat = jnp.concatenate([q * (HEAD_DIM ** -0.5), kc, ks, kw, vc, vs, vw, gate, u], axis=1)
    return cat.astype(BF16), brg.astype(BF16)


def kernel(x, c, w_ada, b_ada, g_pre_mix, g_post_mix, w_in, cmp_pe_k, cmp_w1_k, cmp_w2_k,
           cmp_pe_v, cmp_w1_v, cmp_w2_v, w_pool, pool_scale, w_br_nsa, w_br_pool, w_out,
           g_pre_mlp, g_post_mlp, w_fc1, w_fc2):
    b, s, d = x.shape
    depth = w_ada.shape[0]
    n_sel = s // SEL_BLOCK
    nc = s // CMP_STRIDE
    assert d == D_MODEL and s % TM_PROJ == 0 and s >= WINDOW + TQ
    assert SEL_TOPK <= n_sel <= LANES - HEAD_DIM
    cos_t, sin_t = _rope_tables(s)
    ovl = _overlap_table(nc, n_sel)
    half = CMP_STRIDE * HEAD_DIM

    for l in range(depth):
        mod = _modulation(c, w_ada[l], b_ada[l]).reshape(b, N_MOD, d)
        w_cat, w_brg = _pack_in_weights(w_in[l])
        q, kc, vc, ks_aug, kw_aug, vs_p, vw_p, gl, u = _inproj(
            x, mod, g_pre_mix[l].reshape(1, d), w_cat, cos_t, sin_t)
        kc_aug, vc_p = _compress(
            kc.reshape(b, N_KV, nc, half), vc.reshape(b, N_KV, nc, half),
            cmp_pe_k[l].reshape(2, half), cmp_pe_v[l].reshape(2, half),
            cmp_w1_k[l].reshape(2, half, HEAD_DIM).astype(BF16),
            cmp_w1_v[l].reshape(2, half, HEAD_DIM).astype(BF16),
            cmp_w2_k[l].astype(BF16), cmp_w2_v[l].astype(BF16))
        o_nsa = _attention(q, kc_aug, vc_p, ks_aug, vs_p, kw_aug, vw_p, gl, ovl)
        x = _merge(x, mod, g_pre_mix[l].reshape(1, d), g_post_mix[l].reshape(1, d), o_nsa, u,
                   w_brg, w_pool[l].astype(BF16), pool_scale[l].reshape(1, POOL_WIDTH),
                   w_br_nsa[l].astype(BF16), w_br_pool[l].astype(BF16), w_out[l].astype(BF16))
        x = _mlp(x, mod, g_pre_mlp[l].reshape(1, d), g_post_mlp[l].reshape(1, d),
                 w_fc1[l].astype(BF16), w_fc2[l].astype(BF16))
    return x
```

```python
import functools

import numpy as np
import jax
import jax.numpy as jnp
from jax import lax
from jax.experimental import pallas as pl
from jax.experimental.pallas import tpu as pltpu

F32 = jnp.float32
BF16 = jnp.bfloat16

D_MODEL = 1024
N_HEADS = 8
N_KV = 2
HEAD_DIM = 64
GQA_R = N_HEADS // N_KV
NSA_WIDTH = N_HEADS * HEAD_DIM
KV_WIDTH = N_KV * HEAD_DIM
CMP_BLOCK = 32
CMP_STRIDE = 16
SEL_BLOCK = 64
SEL_TOPK = 16
N_FORCED_LOCAL = 2
WINDOW = 512
FORCE_SCORE = 1e9
POOL_SIZES = (2, 4, 8, 16)
POOL_GROUP = 128
POOL_WIDTH = POOL_GROUP * len(POOL_SIZES)
D_FF = 4 * D_MODEL
ROPE_THETA = 10000.0
EPS = 1e-6
N_MOD = 6
IN_SIZES = (NSA_WIDTH, KV_WIDTH, KV_WIDTH, KV_WIDTH, KV_WIDTH, KV_WIDTH, KV_WIDTH,
            3 * N_HEADS, POOL_WIDTH, 2 * D_MODEL)
IN_SPLITS = [int(v) for v in np.cumsum(IN_SIZES)[:-1]]

LANES = 128
SUBLANES = 8
N_BLK = LANES - HEAD_DIM
POOL_HALO = 16
BLOCK_MASK_BIAS = -float(2 ** 30)
NOT_CAUSAL_SCORE = -1e30

C_Q = 0
C_KC = C_Q + NSA_WIDTH
C_KS = C_KC + KV_WIDTH
C_KW = C_KS + KV_WIDTH
C_VC = C_KW + KV_WIDTH
C_VS = C_VC + KV_WIDTH
C_VW = C_VS + KV_WIDTH
C_GATE = C_VW + KV_WIDTH
C_U = C_GATE + N_KV * LANES
IN_COLS = C_U + POOL_WIDTH

TM_PROJ = 512
TQ = 128
TK_SEL = 512
FF_CHUNK = 1024
VMEM_LIMIT = 56 * 1024 * 1024


def _cparams(n_axes):
    return pltpu.CompilerParams(dimension_semantics=("arbitrary",) * n_axes,
                                vmem_limit_bytes=VMEM_LIMIT)


def _dot(a, b):
    return jnp.dot(a, b, preferred_element_type=F32)


def _dot_nt(a, b):
    return lax.dot_general(a, b, (((1,), (1,)), ((), ())), preferred_element_type=F32)


def _rms(x):
    return x * lax.rsqrt(jnp.mean(x * x, axis=-1, keepdims=True) + EPS)


def _mod_kernel(c_ref, w_ref, b_ref, o_ref):
    c = c_ref[...]
    a = c * jax.nn.sigmoid(c)
    o_ref[...] = jnp.dot(a, w_ref[...], preferred_element_type=F32,
                         precision=lax.Precision.HIGHEST) + b_ref[...]


def _modulation(c, w_ada, b_ada):
    b, d = c.shape
    n = w_ada.shape[1]
    tn = 1024
    return pl.pallas_call(
        _mod_kernel,
        grid=(n // tn,),
        in_specs=[pl.BlockSpec((b, d), lambda j: (0, 0)),
                  pl.BlockSpec((d, tn), lambda j: (0, j)),
                  pl.BlockSpec((1, tn), lambda j: (0, j))],
        out_specs=pl.BlockSpec((b, tn), lambda j: (0, j)),
        out_shape=jax.ShapeDtypeStruct((b, n), F32),
        compiler_params=_cparams(1),
        name="modulation",
    )(c, w_ada, b_ada.reshape(1, n))


def _inproj_kernel(x_ref, mod_ref, g_ref, w_ref, cos_ref, sin_ref,
                   q_ref, kc_ref, vc_ref, ks_ref, kw_ref, vs_ref, vw_ref, gl_ref, u_ref):
    tm = x_ref.shape[1]
    h = (_rms(x_ref[0]) * g_ref[...]) * (1.0 + mod_ref[0, 1:2, :]) + mod_ref[0, 0:1, :]
    p = _dot(h.astype(BF16), w_ref[...])

    lane = lax.broadcasted_iota(jnp.int32, (tm, LANES), 1)
    lo = lane < HEAD_DIM
    first_half = (lane % HEAD_DIM) < (HEAD_DIM // 2)
    cos = cos_ref[...]
    sin = sin_ref[...]

    def rope(v):
        rot = jnp.where(first_half, pltpu.roll(v, LANES - HEAD_DIM // 2, 1),
                        pltpu.roll(v, HEAD_DIM // 2, 1))
        return v * cos + rot * sin

    def swap(v):
        return pltpu.roll(v, HEAD_DIM, 1)

    def slab(c0):
        return p[:, c0:c0 + LANES]

    for j in range(N_HEADS // 2):
        v = rope(slab(C_Q + j * LANES))
        q_ref[0, 2 * j] = jnp.where(lo, v, 0.0).astype(BF16)
        q_ref[0, 2 * j + 1] = jnp.where(lo, swap(v), 0.0).astype(BF16)

    v = rope(slab(C_KC))
    kc_ref[0, 0] = v[:, :HEAD_DIM]
    kc_ref[0, 1] = v[:, HEAD_DIM:]
    v = slab(C_VC)
    vc_ref[0, 0] = v[:, :HEAD_DIM]
    vc_ref[0, 1] = v[:, HEAD_DIM:]

    pos = pl.program_id(1) * tm + lax.broadcasted_iota(jnp.int32, (tm, LANES), 0)
    onehot = jnp.where(lane - HEAD_DIM == pos // SEL_BLOCK, 1.0, 0.0)
    v = rope(slab(C_KS))
    ks_ref[0, 0] = jnp.where(lo, v, onehot).astype(BF16)
    ks_ref[0, 1] = jnp.where(lo, swap(v), onehot).astype(BF16)
    v = rope(slab(C_KW))
    kw_ref[0, 0] = jnp.where(lo, v, 0.0).astype(BF16)
    kw_ref[0, 1] = jnp.where(lo, swap(v), 0.0).astype(BF16)

    for c0, ref in ((C_VS, vs_ref), (C_VW, vw_ref)):
        v = slab(c0)
        sv = swap(v)
        ref[0, 0, 0] = jnp.where(lo, v, 0.0).astype(BF16)
        ref[0, 0, 1] = jnp.where(lo, 0.0, sv).astype(BF16)
        ref[0, 1, 0] = jnp.where(lo, sv, 0.0).astype(BF16)
        ref[0, 1, 1] = jnp.where(lo, 0.0, v).astype(BF16)

    for g in range(N_KV):
        gl_ref[0, g] = slab(C_GATE + g * LANES)
    u_ref[0] = p[:, C_U:C_U + POOL_WIDTH]


def _inproj(x, mod, g_pre, w_cat, cos_t, sin_t):
    b, s, d = x.shape
    tm = min(TM_PROJ, s)
    nt = s // tm
    const = lambda bi, i: (0, 0)
    out_shape = (
        jax.ShapeDtypeStruct((b, N_HEADS, s, LANES), BF16),
        jax.ShapeDtypeStruct((b, N_KV, s, HEAD_DIM), F32),
        jax.ShapeDtypeStruct((b, N_KV, s, HEAD_DIM), F32),
        jax.ShapeDtypeStruct((b, N_KV, s, LANES), BF16),
        jax.ShapeDtypeStruct((b, N_KV, s, LANES), BF16),
        jax.ShapeDtypeStruct((b, N_KV, 2, s, LANES), BF16),
        jax.ShapeDtypeStruct((b, N_KV, 2, s, LANES), BF16),
        jax.ShapeDtypeStruct((b, N_KV, s, LANES), F32),
        jax.ShapeDtypeStruct((b, s, POOL_WIDTH), F32),
    )
    kv4 = lambda w: pl.BlockSpec((1, N_KV, tm, w), lambda bi, i: (bi, 0, i, 0))
    kv5 = pl.BlockSpec((1, N_KV, 2, tm, LANES), lambda bi, i: (bi, 0, 0, i, 0))
    out_specs = (
        pl.BlockSpec((1, N_HEADS, tm, LANES), lambda bi, i: (bi, 0, i, 0)),
        kv4(HEAD_DIM), kv4(HEAD_DIM), kv4(LANES), kv4(LANES), kv5, kv5, kv4(LANES),
        pl.BlockSpec((1, tm, POOL_WIDTH), lambda bi, i: (bi, i, 0)),
    )
    return pl.pallas_call(
        _inproj_kernel,
        grid=(b, nt),
        in_specs=[pl.BlockSpec((1, tm, d), lambda bi, i: (bi, i, 0)),
                  pl.BlockSpec((1, N_MOD, d), lambda bi, i: (bi, 0, 0)),
                  pl.BlockSpec((1, d), const),
                  pl.BlockSpec((d, IN_COLS), const),
                  pl.BlockSpec((tm, LANES), lambda bi, i: (i, 0)),
                  pl.BlockSpec((tm, LANES), lambda bi, i: (i, 0))],
        out_specs=out_specs,
        out_shape=out_shape,
        compiler_params=_cparams(2),
        name="inproj",
    )(x, mod, g_pre, w_cat, cos_t, sin_t)


def _gelu_tanh(x):
    return 0.5 * x * (1.0 + jnp.tanh(np.sqrt(2.0 / np.pi) * (x + 0.044715 * (x * x * x))))


def _compress_kernel(k_ref, v_ref, pek_ref, pev_ref, w1k_ref, w1v_ref, w2k_ref, w2v_ref,
                     kc_ref, vc_ref):
    nc = k_ref.shape[2]

    def run(a, pe_ref, w1_ref, w2_ref):
        pa = _dot((a + pe_ref[0:1, :]).astype(BF16), w1_ref[0])
        pb = _dot((a + pe_ref[1:2, :]).astype(BF16), w1_ref[1])
        pb = jnp.concatenate([pb[1:], jnp.zeros((1, HEAD_DIM), F32)], axis=0)
        hid = _gelu_tanh(pa + pb)
        return _dot(hid.astype(BF16), w2_ref[...])

    zeros = jnp.zeros((nc, HEAD_DIM), F32)
    kc = run(k_ref[0, 0], pek_ref, w1k_ref, w2k_ref)
    kc_ref[0, 0] = jnp.concatenate([kc, zeros], axis=1).astype(BF16)
    vc = run(v_ref[0, 0], pev_ref, w1v_ref, w2v_ref)
    vc_ref[0, 0, 0] = jnp.concatenate([vc, zeros], axis=1).astype(BF16)
    vc_ref[0, 0, 1] = jnp.concatenate([zeros, vc], axis=1).astype(BF16)


def _compress(kc4, vc4, pek, pev, w1k, w1v, w2k, w2v):
    b, g, nc, width = kc4.shape
    half = CMP_STRIDE * HEAD_DIM
    blk = pl.BlockSpec((1, 1, nc, width), lambda bi, gi: (bi, gi, 0, 0))
    c2 = lambda bi, gi: (0, 0)
    c3 = lambda bi, gi: (0, 0, 0)
    return pl.pallas_call(
        _compress_kernel,
        grid=(b, g),
        in_specs=[blk, blk,
                  pl.BlockSpec((2, half), c2), pl.BlockSpec((2, half), c2),
                  pl.BlockSpec((2, half, HEAD_DIM), c3), pl.BlockSpec((2, half, HEAD_DIM), c3),
                  pl.BlockSpec((HEAD_DIM, HEAD_DIM), c2), pl.BlockSpec((HEAD_DIM, HEAD_DIM), c2)],
        out_specs=(pl.BlockSpec((1, 1, nc, LANES), lambda bi, gi: (bi, gi, 0, 0)),
                   pl.BlockSpec((1, 1, 2, nc, LANES), lambda bi, gi: (bi, gi, 0, 0, 0))),
        out_shape=(jax.ShapeDtypeStruct((b, g, nc, LANES), BF16),
                   jax.ShapeDtypeStruct((b, g, 2, nc, LANES), BF16)),
        compiler_params=_cparams(2),
        name="compress",
    )(kc4, vc4, pek, pev, w1k, w1v, w2k, w2v)


def _pair(lo, a, b):
    return jnp.where(lo, a, b)


def _attn_kernel(q_ref, kc_ref, vc_ref, ks_ref, vs_ref, kw_ref, vw_ref, gl_ref, ovl_ref, o_ref):
    tq = q_ref.shape[2]
    nc = kc_ref.shape[2]
    s_len = ks_ref.shape[2]
    rows = GQA_R * tq
    q0 = pl.program_id(2) * tq

    q4 = q_ref[0]
    t_col = q0 + lax.broadcasted_iota(jnp.int32, (tq, 1), 0)
    t_rows = jnp.concatenate([t_col] * GQA_R, axis=0)
    lane = lax.broadcasted_iota(jnp.int32, (tq, LANES), 1)
    lo = lane < HEAD_DIM

    def pv_pairs(pb, v_lo, v_hi):
        o01 = _dot(pb[0:tq], v_lo) + _dot(pb[tq:2 * tq], v_hi)
        o23 = _dot(pb[2 * tq:3 * tq], v_lo) + _dot(pb[3 * tq:], v_hi)
        return o01, o23

    def pair_cols(col):
        return (_pair(lo, col[0:tq], col[tq:2 * tq]),
                _pair(lo, col[2 * tq:3 * tq], col[3 * tq:]))

    sc = _dot_nt(q4.reshape(rows, LANES), kc_ref[0, 0])
    cmp_end = lax.broadcasted_iota(jnp.int32, (1, nc), 1) * CMP_STRIDE + (CMP_BLOCK - 1)
    sc = jnp.where(cmp_end <= t_rows, sc, -jnp.inf)
    m = jnp.max(sc, axis=-1, keepdims=True)
    m = jnp.where(m == -jnp.inf, 0.0, m)
    e = jnp.exp(sc - m)
    p_cmp = e / jnp.maximum(jnp.sum(e, axis=-1, keepdims=True), 1e-30)
    cmp01, cmp23 = pv_pairs(p_cmp.astype(BF16), vc_ref[0, 0, 0], vc_ref[0, 0, 1])

    p_sum = p_cmp[0:tq] + p_cmp[tq:2 * tq] + p_cmp[2 * tq:3 * tq] + p_cmp[3 * tq:]
    p_hi = p_sum.astype(BF16)
    rem = p_sum - p_hi.astype(F32)
    p_mid = rem.astype(BF16)
    p_lo = (rem - p_mid.astype(F32)).astype(BF16)
    ovl_t = ovl_ref[...]
    imp = _dot_nt(ovl_t, p_hi) + _dot_nt(ovl_t, p_mid) + _dot_nt(ovl_t, p_lo)
    blk = lax.broadcasted_iota(jnp.int32, (N_BLK, tq), 0)
    cur = (q0 + lax.broadcasted_iota(jnp.int32, (1, tq), 1)) // SEL_BLOCK
    causal = blk <= cur
    forced = (blk == 0) | (causal & (blk > cur - N_FORCED_LOCAL))
    val = jnp.where(forced, FORCE_SCORE, jnp.where(causal, imp, NOT_CAUSAL_SCORE))

    n_tiles = N_BLK // SUBLANES
    row8 = lax.broadcasted_iota(jnp.int32, (SUBLANES, tq), 0)
    tiles = [val[a * SUBLANES:(a + 1) * SUBLANES] for a in range(n_tiles)]
    ranks = [jnp.zeros((SUBLANES, tq), F32) for _ in range(n_tiles)]
    for mp in range(N_BLK):
        b = jnp.broadcast_to(val[mp:mp + 1], (SUBLANES, tq))
        for a in range(n_tiles):
            if mp < a * SUBLANES:
                inc = jnp.where(b >= tiles[a], 1.0, 0.0)
            elif mp >= (a + 1) * SUBLANES:
                inc = jnp.where(b > tiles[a], 1.0, 0.0)
            else:
                inc = jnp.where(row8 > mp - a * SUBLANES, jnp.where(b >= tiles[a], 1.0, 0.0),
                                jnp.where(b > tiles[a], 1.0, 0.0))
            ranks[a] = ranks[a] + inc
    rank = jnp.concatenate(ranks, axis=0)
    bias_t = jnp.where(rank < float(SEL_TOPK), 0.0, BLOCK_MASK_BIAS)
    bias = jnp.concatenate([jnp.zeros((HEAD_DIM, tq), F32), bias_t], axis=0).T
    q_aug = (q4 + bias.astype(BF16)[None]).reshape(rows, LANES)

    tk = min(TK_SEL, s_len)

    def sel_step(j, carry, diagonal):
        m_i, l_i, a01, a23 = carry
        k0 = pl.multiple_of(j * tk, tk)
        s = _dot_nt(q_aug, ks_ref[0, 0, pl.ds(k0, tk), :])
        if diagonal:
            kpos = k0 + lax.broadcasted_iota(jnp.int32, (1, tk), 1)
            s = jnp.where(kpos <= t_rows, s, -jnp.inf)
        m_new = jnp.maximum(m_i, jnp.max(s, axis=-1, keepdims=True))
        alpha = jnp.exp(m_i - m_new)
        p = jnp.exp(s - m_new)
        l_new = alpha * l_i + jnp.sum(p, axis=-1, keepdims=True)
        o01, o23 = pv_pairs(p.astype(BF16), vs_ref[0, 0, 0, pl.ds(k0, tk), :],
                            vs_ref[0, 0, 1, pl.ds(k0, tk), :])
        al01, al23 = pair_cols(alpha)
        return m_new, l_new, a01 * al01 + o01, a23 * al23 + o23

    n_full = q0 // tk
    zero_acc = jnp.zeros((tq, LANES), F32)
    carry = lax.fori_loop(
        0, n_full, functools.partial(sel_step, diagonal=False),
        (jnp.full((rows, 1), -jnp.inf, F32), jnp.zeros((rows, 1), F32), zero_acc, zero_acc))
    _, l_sel, sel01, sel23 = sel_step(n_full, carry, diagonal=True)

    wlen = min(WINDOW + tq, s_len)
    w0 = pl.multiple_of(jnp.maximum(q0 + tq - wlen, 0), tq)
    sw = _dot_nt(q_aug, kw_ref[0, 0, pl.ds(w0, wlen), :])
    kp = w0 + lax.broadcasted_iota(jnp.int32, (1, wlen), 1)
    sw = jnp.where((kp <= t_rows) & (kp > t_rows - WINDOW), sw, -jnp.inf)
    m_w = jnp.max(sw, axis=-1, keepdims=True)
    e_w = jnp.exp(sw - m_w)
    l_win = jnp.sum(e_w, axis=-1, keepdims=True)
    win01, win23 = pv_pairs(e_w.astype(BF16), vw_ref[0, 0, 0, pl.ds(w0, wlen), :],
                            vw_ref[0, 0, 1, pl.ds(w0, wlen), :])

    sg = jax.nn.sigmoid(gl_ref[0, 0])
    gcol = lambda r, j: sg[:, 3 * r + j:3 * r + j + 1]
    gate = lambda j: jnp.concatenate([gcol(r, j) for r in range(GQA_R)], axis=0)
    gc01, gc23 = pair_cols(gate(0))
    gs01, gs23 = pair_cols(gate(1) / l_sel)
    gw01, gw23 = pair_cols(gate(2) / l_win)
    out01 = gc01 * cmp01 + gs01 * sel01 + gw01 * win01
    out23 = gc23 * cmp23 + gs23 * sel23 + gw23 * win23
    o_ref[0] = jnp.concatenate([out01, out23], axis=1).astype(BF16)


def _attention(q, kc_aug, vc_p, ks_aug, vs_p, kw_aug, vw_p, gl, ovl):
    b, _, s, _ = q.shape
    nc = kc_aug.shape[2]
    tq = min(TQ, s)
    kv = lambda n: pl.BlockSpec((1, 1, n, LANES), lambda bi, gi, i: (bi, gi, 0, 0))
    kv2 = lambda n: pl.BlockSpec((1, 1, 2, n, LANES), lambda bi, gi, i: (bi, gi, 0, 0, 0))
    return pl.pallas_call(
        _attn_kernel,
        grid=(b, N_KV, s // tq),
        in_specs=[pl.BlockSpec((1, GQA_R, tq, LANES), lambda bi, gi, i: (bi, gi, i, 0)),
                  kv(nc), kv2(nc), kv(s), kv2(s), kv(s), kv2(s),
                  pl.BlockSpec((1, 1, tq, LANES), lambda bi, gi, i: (bi, gi, i, 0)),
                  pl.BlockSpec((N_BLK, nc), lambda bi, gi, i: (0, 0))],
        out_specs=pl.BlockSpec((1, tq, GQA_R * HEAD_DIM), lambda bi, gi, i: (bi, i, gi)),
        out_shape=jax.ShapeDtypeStruct((b, s, NSA_WIDTH), BF16),
        compiler_params=_cparams(3),
        name="nsa_attention",
    )(q, kc_aug, vc_p, ks_aug, vs_p, kw_aug, vw_p, gl, ovl)


def _merge_kernel(x_ref, mod_ref, gpre_ref, gpost_ref, on_ref, u_ref, uprev_ref, wbrg_ref,
                  wpool_ref, pscale_ref, wbn_ref, wbp_ref, wout_ref, o_ref):
    tm = x_ref.shape[1]
    i = pl.program_id(1)
    x = x_ref[0]
    h = (_rms(x) * gpre_ref[...]) * (1.0 + mod_ref[0, 1:2, :]) + mod_ref[0, 0:1, :]
    br = jax.nn.sigmoid(_dot(h.astype(BF16), wbrg_ref[...]))

    u = u_ref[0]
    prev = jnp.where(i > 0, uprev_ref[0], 0.0)
    ext = jnp.concatenate([prev, u], axis=0)
    tpos = i * tm + lax.broadcasted_iota(jnp.int32, (tm, 1), 0)
    mixed = []
    for gidx, w in enumerate(POOL_SIZES):
        acc = ext[:, gidx * POOL_GROUP:(gidx + 1) * POOL_GROUP]
        span = 1
        while span < w:
            acc = acc + pltpu.roll(acc, span, 0)
            span *= 2
        cnt = jnp.minimum(tpos + 1, w).astype(F32)
        pooled = acc[POOL_HALO:] / cnt - u[:, gidx * POOL_GROUP:(gidx + 1) * POOL_GROUP]
        mixed.append(_dot(pooled.astype(BF16), wpool_ref[gidx]))
    o_pool = jnp.concatenate(mixed, axis=1) * pscale_ref[...]

    merged = (br[:, :D_MODEL] * _dot(on_ref[0], wbn_ref[...])
              + br[:, D_MODEL:] * _dot(o_pool.astype(BF16), wbp_ref[...]))
    y = _dot(merged.astype(BF16), wout_ref[...])
    o_ref[0] = x + mod_ref[0, 2:3, :] * (_rms(y) * gpost_ref[...])


def _merge(x, mod, g_pre, g_post, o_nsa, u, w_brg, w_pool, pool_scale, w_br_nsa, w_br_pool, w_out):
    b, s, d = x.shape
    tm = min(TM_PROJ, s)
    halo_per_tile = tm // POOL_HALO
    tile = lambda w: pl.BlockSpec((1, tm, w), lambda bi, i: (bi, i, 0))
    c2 = lambda shape: pl.BlockSpec(shape, lambda bi, i: (0, 0))
    return pl.pallas_call(
        _merge_kernel,
        grid=(b, s // tm),
        in_specs=[tile(d),
                  pl.BlockSpec((1, N_MOD, d), lambda bi, i: (bi, 0, 0)),
                  c2((1, d)), c2((1, d)),
                  tile(NSA_WIDTH), tile(POOL_WIDTH),
                  pl.BlockSpec((1, POOL_HALO, POOL_WIDTH),
                               lambda bi, i: (bi, jnp.maximum(i * halo_per_tile - 1, 0), 0)),
                  c2((d, 2 * d)),
                  pl.BlockSpec((len(POOL_SIZES), POOL_GROUP, POOL_GROUP), lambda bi, i: (0, 0, 0)),
                  c2((1, POOL_WIDTH)),
                  c2((NSA_WIDTH, d)), c2((POOL_WIDTH, d)), c2((d, d))],
        out_specs=tile(d),
        out_shape=jax.ShapeDtypeStruct((b, s, d), F32),
        compiler_params=_cparams(2),
        name="merge",
    )(x, mod, g_pre, g_post, o_nsa, u, u, w_brg, w_pool, pool_scale, w_br_nsa, w_br_pool, w_out)


def _mlp_kernel(x_ref, mod_ref, gpre_ref, gpost_ref, w1_ref, w2_ref, o_ref):
    x = x_ref[0]
    h = ((_rms(x) * gpre_ref[...]) * (1.0 + mod_ref[0, 4:5, :]) + mod_ref[0, 3:4, :]).astype(BF16)
    y = jnp.zeros(x.shape, F32)
    for c0 in range(0, D_FF, FF_CHUNK):
        a = jnp.maximum(_dot(h, w1_ref[:, c0:c0 + FF_CHUNK]), 0.0)
        y = y + _dot((a * a).astype(BF16), w2_ref[c0:c0 + FF_CHUNK, :])
    o_ref[0] = x + mod_ref[0, 5:6, :] * (_rms(y) * gpost_ref[...])


def _mlp(x, mod, g_pre, g_post, w_fc1, w_fc2):
    b, s, d = x.shape
    tm = min(TM_PROJ, s)
    tile = pl.BlockSpec((1, tm, d), lambda bi, i: (bi, i, 0))
    resident = lambda shape: pl.BlockSpec(shape, lambda bi, i: (0, 0), pipeline_mode=pl.Buffered(1))
    return pl.pallas_call(
        _mlp_kernel,
        grid=(b, s // tm),
        in_specs=[tile,
                  pl.BlockSpec((1, N_MOD, d), lambda bi, i: (bi, 0, 0)),
                  pl.BlockSpec((1, d), lambda bi, i: (0, 0)),
                  pl.BlockSpec((1, d), lambda bi, i: (0, 0)),
                  resident((d, D_FF)), resident((D_FF, d))],
        out_specs=tile,
        out_shape=jax.ShapeDtypeStruct((b, s, d), F32),
        compiler_params=_cparams(2),
        name="mlp",
    )(x, mod, g_pre, g_post, w_fc1, w_fc2)


def _rope_tables(s):
    half = HEAD_DIM // 2
    inv = ROPE_THETA ** (-np.arange(half, dtype=np.float32) / half)
    ang = np.arange(s, dtype=np.float32)[:, None] * inv[None, :].astype(np.float32)
    cos = np.cos(ang).astype(np.float32)
    sin = np.sin(ang).astype(np.float32)
    reps = LANES // HEAD_DIM
    cos_t = np.tile(np.concatenate([cos, cos], axis=1), (1, reps))
    sin_t = np.tile(np.concatenate([-sin, sin], axis=1), (1, reps))
    return jnp.asarray(cos_t), jnp.asarray(sin_t)


def _overlap_table(nc, n_sel):
    cmp_start = np.arange(nc) * CMP_STRIDE
    sel_start = np.arange(n_sel) * SEL_BLOCK
    ovl = ((cmp_start[None, :] < sel_start[:, None] + SEL_BLOCK)
           & (cmp_start[None, :] + CMP_BLOCK > sel_start[:, None])).astype(np.float32)
    out = np.zeros((N_BLK, nc), np.float32)
    out[:n_sel] = ovl
    return jnp.asarray(out, dtype=BF16)


def _pack_in_weights(w_in):
    q, kc, vc, ks, vs, kw, vw, gate, u, brg = jnp.split(w_in, IN_SPLITS, axis=-1)
    d = w_in.shape[0]
    gate = gate.reshape(d, N_KV, GQA_R * 3)
    gate = jnp.pad(gate, ((0, 0), (0, 0), (0, LANES - GQA_R * 3))).reshape(d, N_KV * LANES)
    cat = jnp.concatenate([q * (HEAD_DIM ** -0.5), kc, ks, kw, vc, vs, vw, gate, u], axis=1)
    return cat.astype(BF16), brg.astype(BF16)


def kernel(x, c, w_ada, b_ada, g_pre_mix, g_post_mix, w_in, cmp_pe_k, cmp_w1_k, cmp_w2_k,
           cmp_pe_v, cmp_w1_v, cmp_w2_v, w_pool, pool_scale, w_br_nsa, w_br_pool, w_out,
           g_pre_mlp, g_post_mlp, w_fc1, w_fc2):
    b, s, d = x.shape
    depth = w_ada.shape[0]
    n_sel = s // SEL_BLOCK
    nc = s // CMP_STRIDE
    assert d == D_MODEL and s % TM_PROJ == 0 and s >= WINDOW + TQ
    assert SEL_TOPK <= n_sel <= N_BLK
    cos_t, sin_t = _rope_tables(s)
    ovl = _overlap_table(nc, n_sel)
    half = CMP_STRIDE * HEAD_DIM

    for l in range(depth):
        mod = _modulation(c, w_ada[l], b_ada[l]).reshape(b, N_MOD, d)
        w_cat, w_brg = _pack_in_weights(w_in[l])
        q, kc, vc, ks_aug, kw_aug, vs_p, vw_p, gl, u = _inproj(
            x, mod, g_pre_mix[l].reshape(1, d), w_cat, cos_t, sin_t)
        kc_aug, vc_p = _compress(
            kc.reshape(b, N_KV, nc, half), vc.reshape(b, N_KV, nc, half),
            cmp_pe_k[l].reshape(2, half), cmp_pe_v[l].reshape(2, half),
            cmp_w1_k[l].reshape(2, half, HEAD_DIM).astype(BF16),
            cmp_w1_v[l].reshape(2, half, HEAD_DIM).astype(BF16),
            cmp_w2_k[l].astype(BF16), cmp_w2_v[l].astype(BF16))
        o_nsa = _attention(q, kc_aug, vc_p, ks_aug, vs_p, kw_aug, vw_p, gl, ovl)
        x = _merge(x, mod, g_pre_mix[l].reshape(1, d), g_post_mix[l].reshape(1, d), o_nsa, u,
                   w_brg, w_pool[l].astype(BF16), pool_scale[l].reshape(1, POOL_WIDTH),
                   w_br_nsa[l].astype(BF16), w_br_pool[l].astype(BF16), w_out[l].astype(BF16))
        x = _mlp(x, mod, g_pre_mlp[l].reshape(1, d), g_post_mlp[l].reshape(1, d),
                 w_fc1[l].astype(BF16), w_fc2[l].astype(BF16))
    return x
```

```python
import functools

import numpy as np
import jax
import jax.numpy as jnp
from jax import lax
from jax.experimental import pallas as pl
from jax.experimental.pallas import tpu as pltpu

F32 = jnp.float32
BF16 = jnp.bfloat16

D_MODEL = 1024
N_HEADS = 8
N_KV = 2
HEAD_DIM = 64
GQA_R = N_HEADS // N_KV
NSA_WIDTH = N_HEADS * HEAD_DIM
KV_WIDTH = N_KV * HEAD_DIM
CMP_BLOCK = 32
CMP_STRIDE = 16
SEL_BLOCK = 64
SEL_TOPK = 16
N_FORCED_LOCAL = 2
WINDOW = 512
FORCE_SCORE = 1e9
POOL_SIZES = (2, 4, 8, 16)
POOL_GROUP = 128
POOL_WIDTH = POOL_GROUP * len(POOL_SIZES)
D_FF = 4 * D_MODEL
ROPE_THETA = 10000.0
EPS = 1e-6
N_MOD = 6
IN_SIZES = (NSA_WIDTH, KV_WIDTH, KV_WIDTH, KV_WIDTH, KV_WIDTH, KV_WIDTH, KV_WIDTH,
            3 * N_HEADS, POOL_WIDTH, 2 * D_MODEL)
IN_SPLITS = [int(v) for v in np.cumsum(IN_SIZES)[:-1]]

LANES = 128
SUBLANES = 8
N_BLK = LANES - HEAD_DIM
ROWSUM_LANE = HEAD_DIM
LOG2E = 1.4426950408889634
POOL_HALO = 16
BLOCK_MASK_BIAS = -float(2 ** 30)
NOT_CAUSAL_SCORE = -1e30

C_Q = 0
C_KC = C_Q + NSA_WIDTH
C_KS = C_KC + KV_WIDTH
C_KW = C_KS + KV_WIDTH
C_VC = C_KW + KV_WIDTH
C_VS = C_VC + KV_WIDTH
C_VW = C_VS + KV_WIDTH
C_GATE = C_VW + KV_WIDTH
C_U = C_GATE + N_KV * LANES
IN_COLS = C_U + POOL_WIDTH

TM_PROJ = 512
TQ = 128
TK_SEL = 512
FF_CHUNK = 1024
VMEM_LIMIT = 56 * 1024 * 1024


def _cparams(n_axes):
    return pltpu.CompilerParams(dimension_semantics=("arbitrary",) * n_axes,
                                vmem_limit_bytes=VMEM_LIMIT)


def _dot(a, b):
    return jnp.dot(a, b, preferred_element_type=F32)


def _dot_nt(a, b):
    return lax.dot_general(a, b, (((1,), (1,)), ((), ())), preferred_element_type=F32)


def _rms(x):
    return x * lax.rsqrt(jnp.mean(x * x, axis=-1, keepdims=True) + EPS)


def _mod_kernel(c_ref, w_ref, b_ref, o_ref):
    c = c_ref[...]
    a = c * jax.nn.sigmoid(c)
    o_ref[...] = jnp.dot(a, w_ref[...], preferred_element_type=F32,
                         precision=lax.Precision.HIGHEST) + b_ref[...]


def _modulation(c, w_ada, b_ada):
    b, d = c.shape
    n = w_ada.shape[1]
    tn = 1024
    return pl.pallas_call(
        _mod_kernel,
        grid=(n // tn,),
        in_specs=[pl.BlockSpec((b, d), lambda j: (0, 0)),
                  pl.BlockSpec((d, tn), lambda j: (0, j)),
                  pl.BlockSpec((1, tn), lambda j: (0, j))],
        out_specs=pl.BlockSpec((b, tn), lambda j: (0, j)),
        out_shape=jax.ShapeDtypeStruct((b, n), F32),
        compiler_params=_cparams(1),
        name="modulation",
    )(c, w_ada, b_ada.reshape(1, n))


def _inproj_kernel(x_ref, mod_ref, g_ref, w_ref, cos_ref, sin_ref,
                   q_ref, kc_ref, vc_ref, ks_ref, kw_ref, vs_ref, vw_ref, gl_ref, u_ref):
    tm = x_ref.shape[1]
    h = (_rms(x_ref[0]) * g_ref[...]) * (1.0 + mod_ref[0, 1:2, :]) + mod_ref[0, 0:1, :]
    p = _dot(h.astype(BF16), w_ref[...])

    lane = lax.broadcasted_iota(jnp.int32, (tm, LANES), 1)
    lo = lane < HEAD_DIM
    first_half = (lane % HEAD_DIM) < (HEAD_DIM // 2)
    cos = cos_ref[...]
    sin = sin_ref[...]

    def rope(v):
        rot = jnp.where(first_half, pltpu.roll(v, LANES - HEAD_DIM // 2, 1),
                        pltpu.roll(v, HEAD_DIM // 2, 1))
        return v * cos + rot * sin

    def swap(v):
        return pltpu.roll(v, HEAD_DIM, 1)

    def slab(c0):
        return p[:, c0:c0 + LANES]

    for j in range(N_HEADS // 2):
        v = rope(slab(C_Q + j * LANES))
        q_ref[0, 2 * j] = jnp.where(lo, v, 0.0).astype(BF16)
        q_ref[0, 2 * j + 1] = jnp.where(lo, swap(v), 0.0).astype(BF16)

    v = rope(slab(C_KC))
    kc_ref[0, 0] = v[:, :HEAD_DIM]
    kc_ref[0, 1] = v[:, HEAD_DIM:]
    v = slab(C_VC)
    vc_ref[0, 0] = v[:, :HEAD_DIM]
    vc_ref[0, 1] = v[:, HEAD_DIM:]

    pos = pl.program_id(1) * tm + lax.broadcasted_iota(jnp.int32, (tm, LANES), 0)
    onehot = jnp.where(lane - HEAD_DIM == pos // SEL_BLOCK, 1.0, 0.0)
    v = rope(slab(C_KS))
    ks_ref[0, 0] = jnp.where(lo, v, onehot).astype(BF16)
    ks_ref[0, 1] = jnp.where(lo, swap(v), onehot).astype(BF16)
    v = rope(slab(C_KW))
    kw_ref[0, 0] = jnp.where(lo, v, 0.0).astype(BF16)
    kw_ref[0, 1] = jnp.where(lo, swap(v), 0.0).astype(BF16)

    ones_col = jnp.where(lane == ROWSUM_LANE, 1.0, 0.0)
    for c0, ref in ((C_VS, vs_ref), (C_VW, vw_ref)):
        v = slab(c0)
        ref[0, 0] = jnp.where(lo, v, ones_col).astype(BF16)
        ref[0, 1] = jnp.where(lo, swap(v), ones_col).astype(BF16)

    for g in range(N_KV):
        gl_ref[0, g] = slab(C_GATE + g * LANES)
    u_ref[0] = p[:, C_U:C_U + POOL_WIDTH]


def _inproj(x, mod, g_pre, w_cat, cos_t, sin_t):
    b, s, d = x.shape
    tm = min(TM_PROJ, s)
    nt = s // tm
    const = lambda bi, i: (0, 0)
    out_shape = (
        jax.ShapeDtypeStruct((b, N_HEADS, s, LANES), BF16),
        jax.ShapeDtypeStruct((b, N_KV, s, HEAD_DIM), F32),
        jax.ShapeDtypeStruct((b, N_KV, s, HEAD_DIM), F32),
        jax.ShapeDtypeStruct((b, N_KV, s, LANES), BF16),
        jax.ShapeDtypeStruct((b, N_KV, s, LANES), BF16),
        jax.ShapeDtypeStruct((b, N_KV, s, LANES), BF16),
        jax.ShapeDtypeStruct((b, N_KV, s, LANES), BF16),
        jax.ShapeDtypeStruct((b, N_KV, s, LANES), F32),
        jax.ShapeDtypeStruct((b, s, POOL_WIDTH), F32),
    )
    kv4 = lambda w: pl.BlockSpec((1, N_KV, tm, w), lambda bi, i: (bi, 0, i, 0))
    out_specs = (
        pl.BlockSpec((1, N_HEADS, tm, LANES), lambda bi, i: (bi, 0, i, 0)),
        kv4(HEAD_DIM), kv4(HEAD_DIM), kv4(LANES), kv4(LANES), kv4(LANES), kv4(LANES), kv4(LANES),
        pl.BlockSpec((1, tm, POOL_WIDTH), lambda bi, i: (bi, i, 0)),
    )
    return pl.pallas_call(
        _inproj_kernel,
        grid=(b, nt),
        in_specs=[pl.BlockSpec((1, tm, d), lambda bi, i: (bi, i, 0)),
                  pl.BlockSpec((1, N_MOD, d), lambda bi, i: (bi, 0, 0)),
                  pl.BlockSpec((1, d), const),
                  pl.BlockSpec((d, IN_COLS), const),
                  pl.BlockSpec((tm, LANES), lambda bi, i: (i, 0)),
                  pl.BlockSpec((tm, LANES), lambda bi, i: (i, 0))],
        out_specs=out_specs,
        out_shape=out_shape,
        compiler_params=_cparams(2),
        name="inproj",
    )(x, mod, g_pre, w_cat, cos_t, sin_t)


def _gelu_tanh(x):
    return 0.5 * x * (1.0 + jnp.tanh(np.sqrt(2.0 / np.pi) * (x + 0.044715 * (x * x * x))))


def _compress_kernel(k_ref, v_ref, pek_ref, pev_ref, w1k_ref, w1v_ref, w2k_ref, w2v_ref,
                     kc_ref, vc_ref):
    nc = k_ref.shape[2]

    def run(a, pe_ref, w1_ref, w2_ref):
        pa = _dot((a + pe_ref[0:1, :]).astype(BF16), w1_ref[0])
        pb = _dot((a + pe_ref[1:2, :]).astype(BF16), w1_ref[1])
        pb = jnp.concatenate([pb[1:], jnp.zeros((1, HEAD_DIM), F32)], axis=0)
        hid = _gelu_tanh(pa + pb)
        return _dot(hid.astype(BF16), w2_ref[...])

    zeros = jnp.zeros((nc, HEAD_DIM), F32)
    kc = run(k_ref[0, 0], pek_ref, w1k_ref, w2k_ref)
    kc_ref[0, 0] = jnp.concatenate([kc, zeros], axis=1).astype(BF16)
    vc = run(v_ref[0, 0], pev_ref, w1v_ref, w2v_ref)
    vc_ref[0, 0] = jnp.concatenate([vc, zeros], axis=1).astype(BF16)


def _compress(kc4, vc4, pek, pev, w1k, w1v, w2k, w2v):
    b, g, nc, width = kc4.shape
    half = CMP_STRIDE * HEAD_DIM
    blk = pl.BlockSpec((1, 1, nc, width), lambda bi, gi: (bi, gi, 0, 0))
    c2 = lambda bi, gi: (0, 0)
    c3 = lambda bi, gi: (0, 0, 0)
    return pl.pallas_call(
        _compress_kernel,
        grid=(b, g),
        in_specs=[blk, blk,
                  pl.BlockSpec((2, half), c2), pl.BlockSpec((2, half), c2),
                  pl.BlockSpec((2, half, HEAD_DIM), c3), pl.BlockSpec((2, half, HEAD_DIM), c3),
                  pl.BlockSpec((HEAD_DIM, HEAD_DIM), c2), pl.BlockSpec((HEAD_DIM, HEAD_DIM), c2)],
        out_specs=(pl.BlockSpec((1, 1, nc, LANES), lambda bi, gi: (bi, gi, 0, 0)),
                   pl.BlockSpec((1, 1, nc, LANES), lambda bi, gi: (bi, gi, 0, 0))),
        out_shape=(jax.ShapeDtypeStruct((b, g, nc, LANES), BF16),
                   jax.ShapeDtypeStruct((b, g, nc, LANES), BF16)),
        compiler_params=_cparams(2),
        name="compress",
    )(kc4, vc4, pek, pev, w1k, w1v, w2k, w2v)


def _attn_kernel(q_ref, kc_ref, vc_ref, ks_ref, vs_ref, kw_ref, vw_ref, gl_ref, ovl_ref, o_ref):
    tq = q_ref.shape[2]
    nc = kc_ref.shape[2]
    s_len = ks_ref.shape[2]
    rows = GQA_R * tq
    q0 = pl.program_id(2) * tq

    q4 = q_ref[0]
    t_col = q0 + lax.broadcasted_iota(jnp.int32, (tq, 1), 0)
    t_rows = jnp.concatenate([t_col] * GQA_R, axis=0)

    def head(a, r):
        return a[r * tq:(r + 1) * tq]

    def pv_heads(pb, v):
        return [_dot(head(pb, r), v) for r in range(GQA_R)]

    sc = _dot_nt(q4.reshape(rows, LANES), kc_ref[0, 0])
    cmp_end = lax.broadcasted_iota(jnp.int32, (1, nc), 1) * CMP_STRIDE + (CMP_BLOCK - 1)
    sc = jnp.where(cmp_end <= t_rows, sc, -jnp.inf)
    m = jnp.max(sc, axis=-1, keepdims=True)
    m = jnp.where(m == -jnp.inf, 0.0, m)
    e = jnp.exp2(sc - m)
    p_cmp = e / jnp.maximum(jnp.sum(e, axis=-1, keepdims=True), 1e-30)
    o_cmp = pv_heads(p_cmp.astype(BF16), vc_ref[0, 0])

    p_sum = p_cmp[0:tq] + p_cmp[tq:2 * tq] + p_cmp[2 * tq:3 * tq] + p_cmp[3 * tq:]
    p_hi = p_sum.astype(BF16)
    rem = p_sum - p_hi.astype(F32)
    p_mid = rem.astype(BF16)
    p_lo = (rem - p_mid.astype(F32)).astype(BF16)
    ovl_t = ovl_ref[...]
    imp = _dot_nt(ovl_t, p_hi) + _dot_nt(ovl_t, p_mid) + _dot_nt(ovl_t, p_lo)
    blk = lax.broadcasted_iota(jnp.int32, (N_BLK, tq), 0)
    cur = (q0 + lax.broadcasted_iota(jnp.int32, (1, tq), 1)) // SEL_BLOCK
    causal = blk <= cur
    forced = (blk == 0) | (causal & (blk > cur - N_FORCED_LOCAL))
    val = jnp.where(forced, FORCE_SCORE, jnp.where(causal, imp, NOT_CAUSAL_SCORE))

    n_tiles = N_BLK // SUBLANES
    row8 = lax.broadcasted_iota(jnp.int32, (SUBLANES, tq), 0)
    tiles = [val[a * SUBLANES:(a + 1) * SUBLANES] for a in range(n_tiles)]
    ranks = [jnp.zeros((SUBLANES, tq), F32) for _ in range(n_tiles)]
    for mp in range(N_BLK):
        b = jnp.broadcast_to(val[mp:mp + 1], (SUBLANES, tq))
        for a in range(n_tiles):
            if mp < a * SUBLANES:
                inc = jnp.where(b >= tiles[a], 1.0, 0.0)
            elif mp >= (a + 1) * SUBLANES:
                inc = jnp.where(b > tiles[a], 1.0, 0.0)
            else:
                inc = jnp.where(row8 > mp - a * SUBLANES, jnp.where(b >= tiles[a], 1.0, 0.0),
                                jnp.where(b > tiles[a], 1.0, 0.0))
            ranks[a] = ranks[a] + inc
    rank = jnp.concatenate(ranks, axis=0)
    bias_t = jnp.where(rank < float(SEL_TOPK), 0.0, BLOCK_MASK_BIAS)
    bias = jnp.concatenate([jnp.zeros((HEAD_DIM, tq), F32), bias_t], axis=0).T
    q_aug = (q4 + bias.astype(BF16)[None]).reshape(rows, LANES)

    tk = min(TK_SEL, s_len)

    def weights(s, m):
        return jnp.exp2((s - m).astype(BF16))

    def sel_step(j, carry, diagonal):
        m_i, accs = carry
        k0 = pl.multiple_of(j * tk, tk)
        s = _dot_nt(q_aug, ks_ref[0, 0, pl.ds(k0, tk), :])
        if diagonal:
            kpos = k0 + lax.broadcasted_iota(jnp.int32, (1, tk), 1)
            s = jnp.where(kpos <= t_rows, s, -jnp.inf)
        m_new = jnp.maximum(m_i, jnp.max(s, axis=-1, keepdims=True))
        alpha = jnp.exp2(m_i - m_new)
        outs = pv_heads(weights(s, m_new), vs_ref[0, 0, pl.ds(k0, tk), :])
        return m_new, tuple(head(alpha, r) * accs[r] + outs[r] for r in range(GQA_R))

    n_full = q0 // tk
    zero_acc = jnp.zeros((tq, LANES), F32)
    carry = lax.fori_loop(
        0, n_full, functools.partial(sel_step, diagonal=False),
        (jnp.full((rows, 1), -jnp.inf, F32), (zero_acc,) * GQA_R))
    _, o_sel = sel_step(n_full, carry, diagonal=True)

    wlen = min(WINDOW + tq, s_len)
    w0 = pl.multiple_of(jnp.maximum(q0 + tq - wlen, 0), tq)
    sw = _dot_nt(q_aug, kw_ref[0, 0, pl.ds(w0, wlen), :])
    kp = w0 + lax.broadcasted_iota(jnp.int32, (1, wlen), 1)
    sw = jnp.where((kp <= t_rows) & (kp > t_rows - WINDOW), sw, -jnp.inf)
    o_win = pv_heads(weights(sw, jnp.max(sw, axis=-1, keepdims=True)),
                     vw_ref[0, 0, pl.ds(w0, wlen), :])

    sg = jax.nn.sigmoid(gl_ref[0, 0])
    rowsum = lambda a: a[:, ROWSUM_LANE:ROWSUM_LANE + 1]
    heads = []
    for r in range(GQA_R):
        g_cmp, g_sel, g_win = (sg[:, 3 * r + j:3 * r + j + 1] for j in range(3))
        o = (g_cmp * o_cmp[r] + (g_sel / rowsum(o_sel[r])) * o_sel[r]
             + (g_win / rowsum(o_win[r])) * o_win[r])
        heads.append(o[:, :HEAD_DIM])
    o_ref[0] = jnp.concatenate(heads, axis=1).astype(BF16)


def _attention(q, kc_aug, vc_p, ks_aug, vs_p, kw_aug, vw_p, gl, ovl):
    b, _, s, _ = q.shape
    nc = kc_aug.shape[2]
    tq = min(TQ, s)
    kv = lambda n: pl.BlockSpec((1, 1, n, LANES), lambda bi, gi, i: (bi, gi, 0, 0))
    return pl.pallas_call(
        _attn_kernel,
        grid=(b, N_KV, s // tq),
        in_specs=[pl.BlockSpec((1, GQA_R, tq, LANES), lambda bi, gi, i: (bi, gi, i, 0)),
                  kv(nc), kv(nc), kv(s), kv(s), kv(s), kv(s),
                  pl.BlockSpec((1, 1, tq, LANES), lambda bi, gi, i: (bi, gi, i, 0)),
                  pl.BlockSpec((N_BLK, nc), lambda bi, gi, i: (0, 0))],
        out_specs=pl.BlockSpec((1, tq, GQA_R * HEAD_DIM), lambda bi, gi, i: (bi, i, gi)),
        out_shape=jax.ShapeDtypeStruct((b, s, NSA_WIDTH), BF16),
        compiler_params=_cparams(3),
        name="nsa_attention",
    )(q, kc_aug, vc_p, ks_aug, vs_p, kw_aug, vw_p, gl, ovl)


def _merge_kernel(x_ref, mod_ref, gpre_ref, gpost_ref, on_ref, u_ref, uprev_ref, wbrg_ref,
                  wpool_ref, pscale_ref, wbn_ref, wbp_ref, wout_ref, o_ref):
    tm = x_ref.shape[1]
    i = pl.program_id(1)
    x = x_ref[0]
    h = (_rms(x) * gpre_ref[...]) * (1.0 + mod_ref[0, 1:2, :]) + mod_ref[0, 0:1, :]
    br = jax.nn.sigmoid(_dot(h.astype(BF16), wbrg_ref[...]))

    u = u_ref[0]
    prev = jnp.where(i > 0, uprev_ref[0], 0.0)
    ext = jnp.concatenate([prev, u], axis=0)
    tpos = i * tm + lax.broadcasted_iota(jnp.int32, (tm, 1), 0)
    mixed = []
    for gidx, w in enumerate(POOL_SIZES):
        acc = ext[:, gidx * POOL_GROUP:(gidx + 1) * POOL_GROUP]
        span = 1
        while span < w:
            acc = acc + pltpu.roll(acc, span, 0)
            span *= 2
        cnt = jnp.minimum(tpos + 1, w).astype(F32)
        pooled = acc[POOL_HALO:] / cnt - u[:, gidx * POOL_GROUP:(gidx + 1) * POOL_GROUP]
        mixed.append(_dot(pooled.astype(BF16), wpool_ref[gidx]))
    o_pool = jnp.concatenate(mixed, axis=1) * pscale_ref[...]

    merged = (br[:, :D_MODEL] * _dot(on_ref[0], wbn_ref[...])
              + br[:, D_MODEL:] * _dot(o_pool.astype(BF16), wbp_ref[...]))
    y = _dot(merged.astype(BF16), wout_ref[...])
    o_ref[0] = x + mod_ref[0, 2:3, :] * (_rms(y) * gpost_ref[...])


def _merge(x, mod, g_pre, g_post, o_nsa, u, w_brg, w_pool, pool_scale, w_br_nsa, w_br_pool, w_out):
    b, s, d = x.shape
    tm = min(TM_PROJ, s)
    halo_per_tile = tm // POOL_HALO
    tile = lambda w: pl.BlockSpec((1, tm, w), lambda bi, i: (bi, i, 0))
    c2 = lambda shape: pl.BlockSpec(shape, lambda bi, i: (0, 0))
    return pl.pallas_call(
        _merge_kernel,
        grid=(b, s // tm),
        in_specs=[tile(d),
                  pl.BlockSpec((1, N_MOD, d), lambda bi, i: (bi, 0, 0)),
                  c2((1, d)), c2((1, d)),
                  tile(NSA_WIDTH), tile(POOL_WIDTH),
                  pl.BlockSpec((1, POOL_HALO, POOL_WIDTH),
                               lambda bi, i: (bi, jnp.maximum(i * halo_per_tile - 1, 0), 0)),
                  c2((d, 2 * d)),
                  pl.BlockSpec((len(POOL_SIZES), POOL_GROUP, POOL_GROUP), lambda bi, i: (0, 0, 0)),
                  c2((1, POOL_WIDTH)),
                  c2((NSA_WIDTH, d)), c2((POOL_WIDTH, d)), c2((d, d))],
        out_specs=tile(d),
        out_shape=jax.ShapeDtypeStruct((b, s, d), F32),
        compiler_params=_cparams(2),
        name="merge",
    )(x, mod, g_pre, g_post, o_nsa, u, u, w_brg, w_pool, pool_scale, w_br_nsa, w_br_pool, w_out)


def _mlp_kernel(x_ref, mod_ref, gpre_ref, gpost_ref, w1_ref, w2_ref, o_ref):
    x = x_ref[0]
    h = ((_rms(x) * gpre_ref[...]) * (1.0 + mod_ref[0, 4:5, :]) + mod_ref[0, 3:4, :]).astype(BF16)
    y = jnp.zeros(x.shape, F32)
    for c0 in range(0, D_FF, FF_CHUNK):
        a = jnp.maximum(_dot(h, w1_ref[:, c0:c0 + FF_CHUNK]), 0.0)
        y = y + _dot((a * a).astype(BF16), w2_ref[c0:c0 + FF_CHUNK, :])
    o_ref[0] = x + mod_ref[0, 5:6, :] * (_rms(y) * gpost_ref[...])


def _mlp(x, mod, g_pre, g_post, w_fc1, w_fc2):
    b, s, d = x.shape
    tm = min(TM_PROJ, s)
    tile = pl.BlockSpec((1, tm, d), lambda bi, i: (bi, i, 0))
    resident = lambda shape: pl.BlockSpec(shape, lambda bi, i: (0, 0), pipeline_mode=pl.Buffered(1))
    return pl.pallas_call(
        _mlp_kernel,
        grid=(b, s // tm),
        in_specs=[tile,
                  pl.BlockSpec((1, N_MOD, d), lambda bi, i: (bi, 0, 0)),
                  pl.BlockSpec((1, d), lambda bi, i: (0, 0)),
                  pl.BlockSpec((1, d), lambda bi, i: (0, 0)),
                  resident((d, D_FF)), resident((D_FF, d))],
        out_specs=tile,
        out_shape=jax.ShapeDtypeStruct((b, s, d), F32),
        compiler_params=_cparams(2),
        name="mlp",
    )(x, mod, g_pre, g_post, w_fc1, w_fc2)


def _rope_tables(s):
    half = HEAD_DIM // 2
    inv = ROPE_THETA ** (-jnp.arange(half, dtype=F32) / half)
    ang = jnp.arange(s, dtype=F32)[:, None] * inv[None, :]
    cos = jnp.cos(ang)
    sin = jnp.sin(ang)
    reps = LANES // HEAD_DIM
    cos_t = jnp.tile(jnp.concatenate([cos, cos], axis=1), (1, reps))
    sin_t = jnp.tile(jnp.concatenate([-sin, sin], axis=1), (1, reps))
    return cos_t, sin_t


def _overlap_table(nc, n_sel):
    cmp_start = np.arange(nc) * CMP_STRIDE
    sel_start = np.arange(n_sel) * SEL_BLOCK
    ovl = ((cmp_start[None, :] < sel_start[:, None] + SEL_BLOCK)
           & (cmp_start[None, :] + CMP_BLOCK > sel_start[:, None])).astype(np.float32)
    out = np.zeros((N_BLK, nc), np.float32)
    out[:n_sel] = ovl
    return jnp.asarray(out, dtype=BF16)


def _pack_in_weights(w_in):
    q, kc, vc, ks, vs, kw, vw, gate, u, brg = jnp.split(w_in, IN_SPLITS, axis=-1)
    d = w_in.shape[0]
    gate = gate.reshape(d, N_KV, GQA_R * 3)
    gate = jnp.pad(gate, ((0, 0), (0, 0), (0, LANES - GQA_R * 3))).reshape(d, N_KV * LANES)
    cat = jnp.concatenate([q * (HEAD_DIM ** -0.5 * LOG2E), kc, ks, kw, vc, vs, vw, gate, u], axis=1)
    return cat.astype(BF16), brg.astype(BF16)


def kernel(x, c, w_ada, b_ada, g_pre_mix, g_post_mix, w_in, cmp_pe_k, cmp_w1_k, cmp_w2_k,
           cmp_pe_v, cmp_w1_v, cmp_w2_v, w_pool, pool_scale, w_br_nsa, w_br_pool, w_out,
           g_pre_mlp, g_post_mlp, w_fc1, w_fc2):
    b, s, d = x.shape
    depth = w_ada.shape[0]
    n_sel = s // SEL_BLOCK
    nc = s // CMP_STRIDE
    assert d == D_MODEL and s % TM_PROJ == 0 and s >= WINDOW + TQ
    assert SEL_TOPK <= n_sel <= N_BLK
    cos_t, sin_t = _rope_tables(s)
    ovl = _overlap_table(nc, n_sel)
    half = CMP_STRIDE * HEAD_DIM

    for l in range(depth):
        mod = _modulation(c, w_ada[l], b_ada[l]).reshape(b, N_MOD, d)
        w_cat, w_brg = _pack_in_weights(w_in[l])
        q, kc, vc, ks_aug, kw_aug, vs_p, vw_p, gl, u = _inproj(
            x, mod, g_pre_mix[l].reshape(1, d), w_cat, cos_t, sin_t)
        kc_aug, vc_p = _compress(
            kc.reshape(b, N_KV, nc, half), vc.reshape(b, N_KV, nc, half),
            cmp_pe_k[l].reshape(2, half), cmp_pe_v[l].reshape(2, half),
            cmp_w1_k[l].reshape(2, half, HEAD_DIM).astype(BF16),
            cmp_w1_v[l].reshape(2, half, HEAD_DIM).astype(BF16),
            cmp_w2_k[l].astype(BF16), cmp_w2_v[l].astype(BF16))
        o_nsa = _attention(q, kc_aug, vc_p, ks_aug, vs_p, kw_aug, vw_p, gl, ovl)
        x = _merge(x, mod, g_pre_mix[l].reshape(1, d), g_post_mix[l].reshape(1, d), o_nsa, u,
                   w_brg, w_pool[l].astype(BF16), pool_scale[l].reshape(1, POOL_WIDTH),
                   w_br_nsa[l].astype(BF16), w_br_pool[l].astype(BF16), w_out[l].astype(BF16))
        x = _mlp(x, mod, g_pre_mlp[l].reshape(1, d), g_post_mlp[l].reshape(1, d),
                 w_fc1[l].astype(BF16), w_fc2[l].astype(BF16))
    return x
```

```python
import functools

import numpy as np
import jax
import jax.numpy as jnp
from jax import lax
from jax.experimental import pallas as pl
from jax.experimental.pallas import tpu as pltpu

F32 = jnp.float32
BF16 = jnp.bfloat16

D_MODEL = 1024
N_HEADS = 8
N_KV = 2
HEAD_DIM = 64
GQA_R = N_HEADS // N_KV
NSA_WIDTH = N_HEADS * HEAD_DIM
KV_WIDTH = N_KV * HEAD_DIM
CMP_BLOCK = 32
CMP_STRIDE = 16
SEL_BLOCK = 64
SEL_TOPK = 16
N_FORCED_LOCAL = 2
WINDOW = 512
FORCE_SCORE = 1e9
POOL_SIZES = (2, 4, 8, 16)
POOL_GROUP = 128
POOL_WIDTH = POOL_GROUP * len(POOL_SIZES)
D_FF = 4 * D_MODEL
ROPE_THETA = 10000.0
EPS = 1e-6
N_MOD = 6
IN_SIZES = (NSA_WIDTH, KV_WIDTH, KV_WIDTH, KV_WIDTH, KV_WIDTH, KV_WIDTH, KV_WIDTH,
            3 * N_HEADS, POOL_WIDTH, 2 * D_MODEL)
IN_SPLITS = [int(v) for v in np.cumsum(IN_SIZES)[:-1]]

LANES = 128
SUBLANES = 8
N_BLK = LANES - HEAD_DIM
ROWSUM_LANE = HEAD_DIM
LOG2E = 1.4426950408889634
POOL_HALO = 16
BLOCK_MASK_BIAS = -float(2 ** 30)
NOT_CAUSAL_SCORE = -1e30

C_Q = 0
C_KC = C_Q + NSA_WIDTH
C_KS = C_KC + KV_WIDTH
C_KW = C_KS + KV_WIDTH
C_VC = C_KW + KV_WIDTH
C_VS = C_VC + KV_WIDTH
C_VW = C_VS + KV_WIDTH
C_GATE = C_VW + KV_WIDTH
C_U = C_GATE + N_KV * LANES
IN_COLS = C_U + POOL_WIDTH

TM_PROJ = 512
TQ = 256
TK_SEL = 512
FF_CHUNK = 1024
VMEM_LIMIT = 56 * 1024 * 1024


def _cparams(n_axes):
    return pltpu.CompilerParams(dimension_semantics=("arbitrary",) * n_axes,
                                vmem_limit_bytes=VMEM_LIMIT)


def _dot(a, b):
    return jnp.dot(a, b, preferred_element_type=F32)


def _dot_nt(a, b):
    return lax.dot_general(a, b, (((1,), (1,)), ((), ())), preferred_element_type=F32)


def _rms(x):
    return x * lax.rsqrt(jnp.mean(x * x, axis=-1, keepdims=True) + EPS)


def _mod_kernel(c_ref, w_ref, b_ref, o_ref):
    c = c_ref[...]
    a = c * jax.nn.sigmoid(c)
    o_ref[...] = jnp.dot(a, w_ref[...], preferred_element_type=F32,
                         precision=lax.Precision.HIGHEST) + b_ref[...]


def _modulation(c, w_ada, b_ada):
    b, d = c.shape
    n = w_ada.shape[1]
    tn = 1024
    return pl.pallas_call(
        _mod_kernel,
        grid=(n // tn,),
        in_specs=[pl.BlockSpec((b, d), lambda j: (0, 0)),
                  pl.BlockSpec((d, tn), lambda j: (0, j)),
                  pl.BlockSpec((1, tn), lambda j: (0, j))],
        out_specs=pl.BlockSpec((b, tn), lambda j: (0, j)),
        out_shape=jax.ShapeDtypeStruct((b, n), F32),
        compiler_params=_cparams(1),
        name="modulation",
    )(c, w_ada, b_ada.reshape(1, n))


def _inproj_kernel(x_ref, mod_ref, g_ref, w_ref, cos_ref, sin_ref,
                   q_ref, kc_ref, vc_ref, ks_ref, kw_ref, vs_ref, vw_ref, gl_ref, u_ref):
    tm = x_ref.shape[1]
    h = (_rms(x_ref[0]) * g_ref[...]) * (1.0 + mod_ref[0, 1:2, :]) + mod_ref[0, 0:1, :]
    p = _dot(h.astype(BF16), w_ref[...])

    lane = lax.broadcasted_iota(jnp.int32, (tm, LANES), 1)
    lo = lane < HEAD_DIM
    first_half = (lane % HEAD_DIM) < (HEAD_DIM // 2)
    cos = cos_ref[...]
    sin = sin_ref[...]

    def rope(v):
        rot = jnp.where(first_half, pltpu.roll(v, LANES - HEAD_DIM // 2, 1),
                        pltpu.roll(v, HEAD_DIM // 2, 1))
        return v * cos + rot * sin

    def swap(v):
        return pltpu.roll(v, HEAD_DIM, 1)

    def slab(c0):
        return p[:, c0:c0 + LANES]

    for j in range(N_HEADS // 2):
        v = rope(slab(C_Q + j * LANES))
        q_ref[0, 2 * j] = jnp.where(lo, v, 0.0).astype(BF16)
        q_ref[0, 2 * j + 1] = jnp.where(lo, swap(v), 0.0).astype(BF16)

    v = rope(slab(C_KC))
    kc_ref[0, 0] = v[:, :HEAD_DIM]
    kc_ref[0, 1] = v[:, HEAD_DIM:]
    v = slab(C_VC)
    vc_ref[0, 0] = v[:, :HEAD_DIM]
    vc_ref[0, 1] = v[:, HEAD_DIM:]

    pos = pl.program_id(1) * tm + lax.broadcasted_iota(jnp.int32, (tm, LANES), 0)
    onehot = jnp.where(lane - HEAD_DIM == pos // SEL_BLOCK, 1.0, 0.0)
    v = rope(slab(C_KS))
    ks_ref[0, 0] = jnp.where(lo, v, onehot).astype(BF16)
    ks_ref[0, 1] = jnp.where(lo, swap(v), onehot).astype(BF16)
    v = rope(slab(C_KW))
    kw_ref[0, 0] = jnp.where(lo, v, 0.0).astype(BF16)
    kw_ref[0, 1] = jnp.where(lo, swap(v), 0.0).astype(BF16)

    ones_col = jnp.where(lane == ROWSUM_LANE, 1.0, 0.0)
    for c0, ref in ((C_VS, vs_ref), (C_VW, vw_ref)):
        v = slab(c0)
        ref[0, 0] = jnp.where(lo, v, ones_col).astype(BF16)
        ref[0, 1] = jnp.where(lo, swap(v), ones_col).astype(BF16)

    for g in range(N_KV):
        gl_ref[0, g] = slab(C_GATE + g * LANES)
    u_ref[0] = p[:, C_U:C_U + POOL_WIDTH]


def _inproj(x, mod, g_pre, w_cat, cos_t, sin_t):
    b, s, d = x.shape
    tm = min(TM_PROJ, s)
    nt = s // tm
    const = lambda bi, i: (0, 0)
    out_shape = (
        jax.ShapeDtypeStruct((b, N_HEADS, s, LANES), BF16),
        jax.ShapeDtypeStruct((b, N_KV, s, HEAD_DIM), F32),
        jax.ShapeDtypeStruct((b, N_KV, s, HEAD_DIM), F32),
        jax.ShapeDtypeStruct((b, N_KV, s, LANES), BF16),
        jax.ShapeDtypeStruct((b, N_KV, s, LANES), BF16),
        jax.ShapeDtypeStruct((b, N_KV, s, LANES), BF16),
        jax.ShapeDtypeStruct((b, N_KV, s, LANES), BF16),
        jax.ShapeDtypeStruct((b, N_KV, s, LANES), F32),
        jax.ShapeDtypeStruct((b, s, POOL_WIDTH), F32),
    )
    kv4 = lambda w: pl.BlockSpec((1, N_KV, tm, w), lambda bi, i: (bi, 0, i, 0))
    out_specs = (
        pl.BlockSpec((1, N_HEADS, tm, LANES), lambda bi, i: (bi, 0, i, 0)),
        kv4(HEAD_DIM), kv4(HEAD_DIM), kv4(LANES), kv4(LANES), kv4(LANES), kv4(LANES), kv4(LANES),
        pl.BlockSpec((1, tm, POOL_WIDTH), lambda bi, i: (bi, i, 0)),
    )
    return pl.pallas_call(
        _inproj_kernel,
        grid=(b, nt),
        in_specs=[pl.BlockSpec((1, tm, d), lambda bi, i: (bi, i, 0)),
                  pl.BlockSpec((1, N_MOD, d), lambda bi, i: (bi, 0, 0)),
                  pl.BlockSpec((1, d), const),
                  pl.BlockSpec((d, IN_COLS), const),
                  pl.BlockSpec((tm, LANES), lambda bi, i: (i, 0)),
                  pl.BlockSpec((tm, LANES), lambda bi, i: (i, 0))],
        out_specs=out_specs,
        out_shape=out_shape,
        compiler_params=_cparams(2),
        name="inproj",
    )(x, mod, g_pre, w_cat, cos_t, sin_t)


def _gelu_tanh(x):
    return 0.5 * x * (1.0 + jnp.tanh(np.sqrt(2.0 / np.pi) * (x + 0.044715 * (x * x * x))))


def _compress_kernel(k_ref, v_ref, pek_ref, pev_ref, w1k_ref, w1v_ref, w2k_ref, w2v_ref,
                     kc_ref, vc_ref):
    nc = k_ref.shape[2]

    def run(a, pe_ref, w1_ref, w2_ref):
        pa = _dot((a + pe_ref[0:1, :]).astype(BF16), w1_ref[0])
        pb = _dot((a + pe_ref[1:2, :]).astype(BF16), w1_ref[1])
        pb = jnp.concatenate([pb[1:], jnp.zeros((1, HEAD_DIM), F32)], axis=0)
        hid = _gelu_tanh(pa + pb)
        return _dot(hid.astype(BF16), w2_ref[...])

    zeros = jnp.zeros((nc, HEAD_DIM), F32)
    kc = run(k_ref[0, 0], pek_ref, w1k_ref, w2k_ref)
    kc_ref[0, 0] = jnp.concatenate([kc, zeros], axis=1).astype(BF16)
    vc = run(v_ref[0, 0], pev_ref, w1v_ref, w2v_ref)
    vc_ref[0, 0] = jnp.concatenate([vc, zeros], axis=1).astype(BF16)


def _compress(kc4, vc4, pek, pev, w1k, w1v, w2k, w2v):
    b, g, nc, width = kc4.shape
    half = CMP_STRIDE * HEAD_DIM
    blk = pl.BlockSpec((1, 1, nc, width), lambda bi, gi: (bi, gi, 0, 0))
    c2 = lambda bi, gi: (0, 0)
    c3 = lambda bi, gi: (0, 0, 0)
    return pl.pallas_call(
        _compress_kernel,
        grid=(b, g),
        in_specs=[blk, blk,
                  pl.BlockSpec((2, half), c2), pl.BlockSpec((2, half), c2),
                  pl.BlockSpec((2, half, HEAD_DIM), c3), pl.BlockSpec((2, half, HEAD_DIM), c3),
                  pl.BlockSpec((HEAD_DIM, HEAD_DIM), c2), pl.BlockSpec((HEAD_DIM, HEAD_DIM), c2)],
        out_specs=(pl.BlockSpec((1, 1, nc, LANES), lambda bi, gi: (bi, gi, 0, 0)),
                   pl.BlockSpec((1, 1, nc, LANES), lambda bi, gi: (bi, gi, 0, 0))),
        out_shape=(jax.ShapeDtypeStruct((b, g, nc, LANES), BF16),
                   jax.ShapeDtypeStruct((b, g, nc, LANES), BF16)),
        compiler_params=_cparams(2),
        name="compress",
    )(kc4, vc4, pek, pev, w1k, w1v, w2k, w2v)


def _attn_kernel(q_ref, kc_ref, vc_ref, ks_ref, vs_ref, kw_ref, vw_ref, gl_ref, ovl_ref, o_ref):
    tq = q_ref.shape[2]
    nc = kc_ref.shape[2]
    s_len = ks_ref.shape[2]
    rows = GQA_R * tq
    q0 = pl.program_id(2) * tq

    q4 = q_ref[0]
    t_col = q0 + lax.broadcasted_iota(jnp.int32, (tq, 1), 0)
    t_rows = jnp.concatenate([t_col] * GQA_R, axis=0)

    def head(a, r):
        return a[r * tq:(r + 1) * tq]

    def pv_heads(pb, v):
        return [_dot(head(pb, r), v) for r in range(GQA_R)]

    sc = _dot_nt(q4.reshape(rows, LANES), kc_ref[0, 0])
    cmp_end = lax.broadcasted_iota(jnp.int32, (1, nc), 1) * CMP_STRIDE + (CMP_BLOCK - 1)
    sc = jnp.where(cmp_end <= t_rows, sc, -jnp.inf)
    m = jnp.max(sc, axis=-1, keepdims=True)
    m = jnp.where(m == -jnp.inf, 0.0, m)
    e = jnp.exp2(sc - m)
    p_cmp = e / jnp.maximum(jnp.sum(e, axis=-1, keepdims=True), 1e-30)
    o_cmp = pv_heads(p_cmp.astype(BF16), vc_ref[0, 0])

    p_sum = p_cmp[0:tq] + p_cmp[tq:2 * tq] + p_cmp[2 * tq:3 * tq] + p_cmp[3 * tq:]
    p_hi = p_sum.astype(BF16)
    rem = p_sum - p_hi.astype(F32)
    p_mid = rem.astype(BF16)
    p_lo = (rem - p_mid.astype(F32)).astype(BF16)
    ovl_t = ovl_ref[...]
    imp = _dot_nt(ovl_t, p_hi) + _dot_nt(ovl_t, p_mid) + _dot_nt(ovl_t, p_lo)
    blk = lax.broadcasted_iota(jnp.int32, (N_BLK, tq), 0)
    cur = (q0 + lax.broadcasted_iota(jnp.int32, (1, tq), 1)) // SEL_BLOCK
    causal = blk <= cur
    forced = (blk == 0) | (causal & (blk > cur - N_FORCED_LOCAL))
    val = jnp.where(forced, FORCE_SCORE, jnp.where(causal, imp, NOT_CAUSAL_SCORE))

    n_tiles = N_BLK // SUBLANES
    row8 = lax.broadcasted_iota(jnp.int32, (SUBLANES, tq), 0)
    tiles = [val[a * SUBLANES:(a + 1) * SUBLANES] for a in range(n_tiles)]
    ranks = [jnp.zeros((SUBLANES, tq), F32) for _ in range(n_tiles)]
    for mp in range(N_BLK):
        b = jnp.broadcast_to(val[mp:mp + 1], (SUBLANES, tq))
        for a in range(n_tiles):
            if mp < a * SUBLANES:
                inc = jnp.where(b >= tiles[a], 1.0, 0.0)
            elif mp >= (a + 1) * SUBLANES:
                inc = jnp.where(b > tiles[a], 1.0, 0.0)
            else:
                inc = jnp.where(row8 > mp - a * SUBLANES, jnp.where(b >= tiles[a], 1.0, 0.0),
                                jnp.where(b > tiles[a], 1.0, 0.0))
            ranks[a] = ranks[a] + inc
    rank = jnp.concatenate(ranks, axis=0)
    bias_t = jnp.where(rank < float(SEL_TOPK), 0.0, BLOCK_MASK_BIAS)
    bias = jnp.concatenate([jnp.zeros((HEAD_DIM, tq), F32), bias_t], axis=0).T
    q_aug = (q4 + bias.astype(BF16)[None]).reshape(rows, LANES)

    tk = min(TK_SEL, s_len)

    def weights(s, m):
        return jnp.exp2((s - m).astype(BF16))

    def sel_step(j, carry, diagonal):
        m_i, accs = carry
        k0 = pl.multiple_of(j * tk, tk)
        s = _dot_nt(q_aug, ks_ref[0, 0, pl.ds(k0, tk), :])
        if diagonal:
            kpos = k0 + lax.broadcasted_iota(jnp.int32, (1, tk), 1)
            s = jnp.where(kpos <= t_rows, s, -jnp.inf)
        m_new = jnp.maximum(m_i, jnp.max(s, axis=-1, keepdims=True))
        alpha = jnp.exp2(m_i - m_new)
        outs = pv_heads(weights(s, m_new), vs_ref[0, 0, pl.ds(k0, tk), :])
        return m_new, tuple(head(alpha, r) * accs[r] + outs[r] for r in range(GQA_R))

    n_full = q0 // tk
    zero_acc = jnp.zeros((tq, LANES), F32)
    carry = lax.fori_loop(
        0, n_full, functools.partial(sel_step, diagonal=False),
        (jnp.full((rows, 1), -jnp.inf, F32), (zero_acc,) * GQA_R))
    _, o_sel = sel_step(n_full, carry, diagonal=True)

    wlen = min(WINDOW + tq, s_len)
    w0 = pl.multiple_of(jnp.maximum(q0 + tq - wlen, 0), tq)
    sw = _dot_nt(q_aug, kw_ref[0, 0, pl.ds(w0, wlen), :])
    kp = w0 + lax.broadcasted_iota(jnp.int32, (1, wlen), 1)
    sw = jnp.where((kp <= t_rows) & (kp > t_rows - WINDOW), sw, -jnp.inf)
    o_win = pv_heads(weights(sw, jnp.max(sw, axis=-1, keepdims=True)),
                     vw_ref[0, 0, pl.ds(w0, wlen), :])

    sg = jax.nn.sigmoid(gl_ref[0, 0])
    rowsum = lambda a: a[:, ROWSUM_LANE:ROWSUM_LANE + 1]
    heads = []
    for r in range(GQA_R):
        g_cmp, g_sel, g_win = (sg[:, 3 * r + j:3 * r + j + 1] for j in range(3))
        o = (g_cmp * o_cmp[r] + (g_sel / rowsum(o_sel[r])) * o_sel[r]
             + (g_win / rowsum(o_win[r])) * o_win[r])
        heads.append(o[:, :HEAD_DIM])
    o_ref[0] = jnp.concatenate(heads, axis=1).astype(BF16)


def _attention(q, kc_aug, vc_p, ks_aug, vs_p, kw_aug, vw_p, gl, ovl):
    b, _, s, _ = q.shape
    nc = kc_aug.shape[2]
    tq = min(TQ, s)
    kv = lambda n: pl.BlockSpec((1, 1, n, LANES), lambda bi, gi, i: (bi, gi, 0, 0))
    return pl.pallas_call(
        _attn_kernel,
        grid=(b, N_KV, s // tq),
        in_specs=[pl.BlockSpec((1, GQA_R, tq, LANES), lambda bi, gi, i: (bi, gi, i, 0)),
                  kv(nc), kv(nc), kv(s), kv(s), kv(s), kv(s),
                  pl.BlockSpec((1, 1, tq, LANES), lambda bi, gi, i: (bi, gi, i, 0)),
                  pl.BlockSpec((N_BLK, nc), lambda bi, gi, i: (0, 0))],
        out_specs=pl.BlockSpec((1, tq, GQA_R * HEAD_DIM), lambda bi, gi, i: (bi, i, gi)),
        out_shape=jax.ShapeDtypeStruct((b, s, NSA_WIDTH), BF16),
        compiler_params=_cparams(3),
        name="nsa_attention",
    )(q, kc_aug, vc_p, ks_aug, vs_p, kw_aug, vw_p, gl, ovl)


def _merge_kernel(x_ref, mod_ref, gpre_ref, gpost_ref, on_ref, u_ref, uprev_ref, wbrg_ref,
                  wpool_ref, pscale_ref, wbn_ref, wbp_ref, wout_ref, o_ref):
    tm = x_ref.shape[1]
    i = pl.program_id(1)
    x = x_ref[0]
    h = (_rms(x) * gpre_ref[...]) * (1.0 + mod_ref[0, 1:2, :]) + mod_ref[0, 0:1, :]
    br = jax.nn.sigmoid(_dot(h.astype(BF16), wbrg_ref[...]))

    u = u_ref[0]
    prev = jnp.where(i > 0, uprev_ref[0], 0.0)
    ext = jnp.concatenate([prev, u], axis=0)
    tpos = i * tm + lax.broadcasted_iota(jnp.int32, (tm, 1), 0)
    mixed = []
    for gidx, w in enumerate(POOL_SIZES):
        acc = ext[:, gidx * POOL_GROUP:(gidx + 1) * POOL_GROUP]
        span = 1
        while span < w:
            acc = acc + pltpu.roll(acc, span, 0)
            span *= 2
        cnt = jnp.minimum(tpos + 1, w).astype(F32)
        pooled = acc[POOL_HALO:] / cnt - u[:, gidx * POOL_GROUP:(gidx + 1) * POOL_GROUP]
        mixed.append(_dot(pooled.astype(BF16), wpool_ref[gidx]))
    o_pool = jnp.concatenate(mixed, axis=1) * pscale_ref[...]

    merged = (br[:, :D_MODEL] * _dot(on_ref[0], wbn_ref[...])
              + br[:, D_MODEL:] * _dot(o_pool.astype(BF16), wbp_ref[...]))
    y = _dot(merged.astype(BF16), wout_ref[...])
    o_ref[0] = x + mod_ref[0, 2:3, :] * (_rms(y) * gpost_ref[...])


def _merge(x, mod, g_pre, g_post, o_nsa, u, w_brg, w_pool, pool_scale, w_br_nsa, w_br_pool, w_out):
    b, s, d = x.shape
    tm = min(TM_PROJ, s)
    halo_per_tile = tm // POOL_HALO
    tile = lambda w: pl.BlockSpec((1, tm, w), lambda bi, i: (bi, i, 0))
    c2 = lambda shape: pl.BlockSpec(shape, lambda bi, i: (0, 0))
    return pl.pallas_call(
        _merge_kernel,
        grid=(b, s // tm),
        in_specs=[tile(d),
                  pl.BlockSpec((1, N_MOD, d), lambda bi, i: (bi, 0, 0)),
                  c2((1, d)), c2((1, d)),
                  tile(NSA_WIDTH), tile(POOL_WIDTH),
                  pl.BlockSpec((1, POOL_HALO, POOL_WIDTH),
                               lambda bi, i: (bi, jnp.maximum(i * halo_per_tile - 1, 0), 0)),
                  c2((d, 2 * d)),
                  pl.BlockSpec((len(POOL_SIZES), POOL_GROUP, POOL_GROUP), lambda bi, i: (0, 0, 0)),
                  c2((1, POOL_WIDTH)),
                  c2((NSA_WIDTH, d)), c2((POOL_WIDTH, d)), c2((d, d))],
        out_specs=tile(d),
        out_shape=jax.ShapeDtypeStruct((b, s, d), F32),
        compiler_params=_cparams(2),
        name="merge",
    )(x, mod, g_pre, g_post, o_nsa, u, u, w_brg, w_pool, pool_scale, w_br_nsa, w_br_pool, w_out)


def _mlp_kernel(x_ref, mod_ref, gpre_ref, gpost_ref, w1_ref, w2_ref, o_ref):
    x = x_ref[0]
    h = ((_rms(x) * gpre_ref[...]) * (1.0 + mod_ref[0, 4:5, :]) + mod_ref[0, 3:4, :]).astype(BF16)
    y = jnp.zeros(x.shape, F32)
    for c0 in range(0, D_FF, FF_CHUNK):
        a = jnp.maximum(_dot(h, w1_ref[:, c0:c0 + FF_CHUNK]), 0.0)
        y = y + _dot((a * a).astype(BF16), w2_ref[c0:c0 + FF_CHUNK, :])
    o_ref[0] = x + mod_ref[0, 5:6, :] * (_rms(y) * gpost_ref[...])


def _mlp(x, mod, g_pre, g_post, w_fc1, w_fc2):
    b, s, d = x.shape
    tm = min(TM_PROJ, s)
    tile = pl.BlockSpec((1, tm, d), lambda bi, i: (bi, i, 0))
    resident = lambda shape: pl.BlockSpec(shape, lambda bi, i: (0, 0), pipeline_mode=pl.Buffered(1))
    return pl.pallas_call(
        _mlp_kernel,
        grid=(b, s // tm),
        in_specs=[tile,
                  pl.BlockSpec((1, N_MOD, d), lambda bi, i: (bi, 0, 0)),
                  pl.BlockSpec((1, d), lambda bi, i: (0, 0)),
                  pl.BlockSpec((1, d), lambda bi, i: (0, 0)),
                  resident((d, D_FF)), resident((D_FF, d))],
        out_specs=tile,
        out_shape=jax.ShapeDtypeStruct((b, s, d), F32),
        compiler_params=_cparams(2),
        name="mlp",
    )(x, mod, g_pre, g_post, w_fc1, w_fc2)


def _rope_tables(s):
    half = HEAD_DIM // 2
    inv = ROPE_THETA ** (-jnp.arange(half, dtype=F32) / half)
    ang = jnp.arange(s, dtype=F32)[:, None] * inv[None, :]
    cos = jnp.cos(ang)
    sin = jnp.sin(ang)
    reps = LANES // HEAD_DIM
    cos_t = jnp.tile(jnp.concatenate([cos, cos], axis=1), (1, reps))
    sin_t = jnp.tile(jnp.concatenate([-sin, sin], axis=1), (1, reps))
    return cos_t, sin_t


def _overlap_table(nc, n_sel):
    cmp_start = np.arange(nc) * CMP_STRIDE
    sel_start = np.arange(n_sel) * SEL_BLOCK
    ovl = ((cmp_start[None, :] < sel_start[:, None] + SEL_BLOCK)
           & (cmp_start[None, :] + CMP_BLOCK > sel_start[:, None])).astype(np.float32)
    out = np.zeros((N_BLK, nc), np.float32)
    out[:n_sel] = ovl
    return jnp.asarray(out, dtype=BF16)


def _pack_in_weights(w_in):
    q, kc, vc, ks, vs, kw, vw, gate, u, brg = jnp.split(w_in, IN_SPLITS, axis=-1)
    d = w_in.shape[0]
    gate = gate.reshape(d, N_KV, GQA_R * 3)
    gate = jnp.pad(gate, ((0, 0), (0, 0), (0, LANES - GQA_R * 3))).reshape(d, N_KV * LANES)
    cat = jnp.concatenate([q * (HEAD_DIM ** -0.5 * LOG2E), kc, ks, kw, vc, vs, vw, gate, u], axis=1)
    return cat.astype(BF16), brg.astype(BF16)


def kernel(x, c, w_ada, b_ada, g_pre_mix, g_post_mix, w_in, cmp_pe_k, cmp_w1_k, cmp_w2_k,
           cmp_pe_v, cmp_w1_v, cmp_w2_v, w_pool, pool_scale, w_br_nsa, w_br_pool, w_out,
           g_pre_mlp, g_post_mlp, w_fc1, w_fc2):
    b, s, d = x.shape
    depth = w_ada.shape[0]
    n_sel = s // SEL_BLOCK
    nc = s // CMP_STRIDE
    assert d == D_MODEL and s % TM_PROJ == 0 and s >= WINDOW + TQ
    assert SEL_TOPK <= n_sel <= N_BLK
    cos_t, sin_t = _rope_tables(s)
    ovl = _overlap_table(nc, n_sel)
    half = CMP_STRIDE * HEAD_DIM

    for l in range(depth):
        mod = _modulation(c, w_ada[l], b_ada[l]).reshape(b, N_MOD, d)
        w_cat, w_brg = _pack_in_weights(w_in[l])
        q, kc, vc, ks_aug, kw_aug, vs_p, vw_p, gl, u = _inproj(
            x, mod, g_pre_mix[l].reshape(1, d), w_cat, cos_t, sin_t)
        kc_aug, vc_p = _compress(
            kc.reshape(b, N_KV, nc, half), vc.reshape(b, N_KV, nc, half),
            cmp_pe_k[l].reshape(2, half), cmp_pe_v[l].reshape(2, half),
            cmp_w1_k[l].reshape(2, half, HEAD_DIM).astype(BF16),
            cmp_w1_v[l].reshape(2, half, HEAD_DIM).astype(BF16),
            cmp_w2_k[l].astype(BF16), cmp_w2_v[l].astype(BF16))
        o_nsa = _attention(q, kc_aug, vc_p, ks_aug, vs_p, kw_aug, vw_p, gl, ovl)
        x = _merge(x, mod, g_pre_mix[l].reshape(1, d), g_post_mix[l].reshape(1, d), o_nsa, u,
                   w_brg, w_pool[l].astype(BF16), pool_scale[l].reshape(1, POOL_WIDTH),
                   w_br_nsa[l].astype(BF16), w_br_pool[l].astype(BF16), w_out[l].astype(BF16))
        x = _mlp(x, mod, g_pre_mlp[l].reshape(1, d), g_post_mlp[l].reshape(1, d),
                 w_fc1[l].astype(BF16), w_fc2[l].astype(BF16))
    return x
```

```python
import functools

import numpy as np
import jax
import jax.numpy as jnp
from jax import lax
from jax.experimental import pallas as pl
from jax.experimental.pallas import tpu as pltpu

F32 = jnp.float32
BF16 = jnp.bfloat16

D_MODEL = 1024
N_HEADS = 8
N_KV = 2
HEAD_DIM = 64
GQA_R = N_HEADS // N_KV
NSA_WIDTH = N_HEADS * HEAD_DIM
KV_WIDTH = N_KV * HEAD_DIM
CMP_BLOCK = 32
CMP_STRIDE = 16
SEL_BLOCK = 64
SEL_TOPK = 16
N_FORCED_LOCAL = 2
WINDOW = 512
FORCE_SCORE = 1e9
POOL_SIZES = (2, 4, 8, 16)
POOL_GROUP = 128
POOL_WIDTH = POOL_GROUP * len(POOL_SIZES)
D_FF = 4 * D_MODEL
ROPE_THETA = 10000.0
EPS = 1e-6
N_MOD = 6
IN_SIZES = (NSA_WIDTH, KV_WIDTH, KV_WIDTH, KV_WIDTH, KV_WIDTH, KV_WIDTH, KV_WIDTH,
            3 * N_HEADS, POOL_WIDTH, 2 * D_MODEL)
IN_SPLITS = [int(v) for v in np.cumsum(IN_SIZES)[:-1]]

LANES = 128
SUBLANES = 8
N_BLK = LANES - HEAD_DIM
ROWSUM_LANE = HEAD_DIM
LOG2E = 1.4426950408889634
POOL_HALO = 16
BLOCK_MASK_BIAS = -float(2 ** 30)
NOT_CAUSAL_SCORE = -1e30

C_Q = 0
C_KC = C_Q + NSA_WIDTH
C_KS = C_KC + KV_WIDTH
C_KW = C_KS + KV_WIDTH
C_VC = C_KW + KV_WIDTH
C_VS = C_VC + KV_WIDTH
C_VW = C_VS + KV_WIDTH
C_GATE = C_VW + KV_WIDTH
C_U = C_GATE + N_KV * LANES
IN_COLS = C_U + POOL_WIDTH

TM_PROJ = 512
TQ = 256
TK_SEL = 1024
TK_SEL_DIAG = 512
FF_CHUNK = 1024
VMEM_LIMIT = 56 * 1024 * 1024


def _cparams(n_axes):
    return pltpu.CompilerParams(dimension_semantics=("arbitrary",) * n_axes,
                                vmem_limit_bytes=VMEM_LIMIT)


def _dot(a, b):
    return jnp.dot(a, b, preferred_element_type=F32)


def _dot_nt(a, b):
    return lax.dot_general(a, b, (((1,), (1,)), ((), ())), preferred_element_type=F32)


def _rms(x):
    return x * lax.rsqrt(jnp.mean(x * x, axis=-1, keepdims=True) + EPS)


def _mod_kernel(c_ref, w_ref, b_ref, o_ref):
    c = c_ref[...]
    a = c * jax.nn.sigmoid(c)
    o_ref[...] = jnp.dot(a, w_ref[...], preferred_element_type=F32,
                         precision=lax.Precision.HIGHEST) + b_ref[...]


def _modulation(c, w_ada, b_ada):
    b, d = c.shape
    n = w_ada.shape[1]
    tn = 1024
    return pl.pallas_call(
        _mod_kernel,
        grid=(n // tn,),
        in_specs=[pl.BlockSpec((b, d), lambda j: (0, 0)),
                  pl.BlockSpec((d, tn), lambda j: (0, j)),
                  pl.BlockSpec((1, tn), lambda j: (0, j))],
        out_specs=pl.BlockSpec((b, tn), lambda j: (0, j)),
        out_shape=jax.ShapeDtypeStruct((b, n), F32),
        compiler_params=_cparams(1),
        name="modulation",
    )(c, w_ada, b_ada.reshape(1, n))


def _inproj_kernel(x_ref, mod_ref, g_ref, w_ref, cos_ref, sin_ref,
                   q_ref, kc_ref, vc_ref, ks_ref, kw_ref, vs_ref, vw_ref, gl_ref, u_ref):
    tm = x_ref.shape[1]
    h = (_rms(x_ref[0]) * g_ref[...]) * (1.0 + mod_ref[0, 1:2, :]) + mod_ref[0, 0:1, :]
    p = _dot(h.astype(BF16), w_ref[...])

    lane = lax.broadcasted_iota(jnp.int32, (tm, LANES), 1)
    lo = lane < HEAD_DIM
    first_half = (lane % HEAD_DIM) < (HEAD_DIM // 2)
    cos = cos_ref[...]
    sin = sin_ref[...]

    def rope(v):
        rot = jnp.where(first_half, pltpu.roll(v, LANES - HEAD_DIM // 2, 1),
                        pltpu.roll(v, HEAD_DIM // 2, 1))
        return v * cos + rot * sin

    def swap(v):
        return pltpu.roll(v, HEAD_DIM, 1)

    def slab(c0):
        return p[:, c0:c0 + LANES]

    for j in range(N_HEADS // 2):
        v = rope(slab(C_Q + j * LANES))
        q_ref[0, 2 * j] = jnp.where(lo, v, 0.0).astype(BF16)
        q_ref[0, 2 * j + 1] = jnp.where(lo, swap(v), 0.0).astype(BF16)

    v = rope(slab(C_KC))
    kc_ref[0, 0] = v[:, :HEAD_DIM]
    kc_ref[0, 1] = v[:, HEAD_DIM:]
    v = slab(C_VC)
    vc_ref[0, 0] = v[:, :HEAD_DIM]
    vc_ref[0, 1] = v[:, HEAD_DIM:]

    pos = pl.program_id(1) * tm + lax.broadcasted_iota(jnp.int32, (tm, LANES), 0)
    onehot = jnp.where(lane - HEAD_DIM == pos // SEL_BLOCK, 1.0, 0.0)
    v = rope(slab(C_KS))
    ks_ref[0, 0] = jnp.where(lo, v, onehot).astype(BF16)
    ks_ref[0, 1] = jnp.where(lo, swap(v), onehot).astype(BF16)
    v = rope(slab(C_KW))
    kw_ref[0, 0] = jnp.where(lo, v, 0.0).astype(BF16)
    kw_ref[0, 1] = jnp.where(lo, swap(v), 0.0).astype(BF16)

    ones_col = jnp.where(lane == ROWSUM_LANE, 1.0, 0.0)
    for c0, ref in ((C_VS, vs_ref), (C_VW, vw_ref)):
        v = slab(c0)
        ref[0, 0] = jnp.where(lo, v, ones_col).astype(BF16)
        ref[0, 1] = jnp.where(lo, swap(v), ones_col).astype(BF16)

    for g in range(N_KV):
        gl_ref[0, g] = slab(C_GATE + g * LANES)
    u_ref[0] = p[:, C_U:C_U + POOL_WIDTH]


def _inproj(x, mod, g_pre, w_cat, cos_t, sin_t):
    b, s, d = x.shape
    tm = min(TM_PROJ, s)
    nt = s // tm
    const = lambda bi, i: (0, 0)
    out_shape = (
        jax.ShapeDtypeStruct((b, N_HEADS, s, LANES), BF16),
        jax.ShapeDtypeStruct((b, N_KV, s, HEAD_DIM), F32),
        jax.ShapeDtypeStruct((b, N_KV, s, HEAD_DIM), F32),
        jax.ShapeDtypeStruct((b, N_KV, s, LANES), BF16),
        jax.ShapeDtypeStruct((b, N_KV, s, LANES), BF16),
        jax.ShapeDtypeStruct((b, N_KV, s, LANES), BF16),
        jax.ShapeDtypeStruct((b, N_KV, s, LANES), BF16),
        jax.ShapeDtypeStruct((b, N_KV, s, LANES), F32),
        jax.ShapeDtypeStruct((b, s, POOL_WIDTH), F32),
    )
    kv4 = lambda w: pl.BlockSpec((1, N_KV, tm, w), lambda bi, i: (bi, 0, i, 0))
    out_specs = (
        pl.BlockSpec((1, N_HEADS, tm, LANES), lambda bi, i: (bi, 0, i, 0)),
        kv4(HEAD_DIM), kv4(HEAD_DIM), kv4(LANES), kv4(LANES), kv4(LANES), kv4(LANES), kv4(LANES),
        pl.BlockSpec((1, tm, POOL_WIDTH), lambda bi, i: (bi, i, 0)),
    )
    return pl.pallas_call(
        _inproj_kernel,
        grid=(b, nt),
        in_specs=[pl.BlockSpec((1, tm, d), lambda bi, i: (bi, i, 0)),
                  pl.BlockSpec((1, N_MOD, d), lambda bi, i: (bi, 0, 0)),
                  pl.BlockSpec((1, d), const),
                  pl.BlockSpec((d, IN_COLS), const),
                  pl.BlockSpec((tm, LANES), lambda bi, i: (i, 0)),
                  pl.BlockSpec((tm, LANES), lambda bi, i: (i, 0))],
        out_specs=out_specs,
        out_shape=out_shape,
        compiler_params=_cparams(2),
        name="inproj",
    )(x, mod, g_pre, w_cat, cos_t, sin_t)


def _gelu_tanh(x):
    return 0.5 * x * (1.0 + jnp.tanh(np.sqrt(2.0 / np.pi) * (x + 0.044715 * (x * x * x))))


def _compress_kernel(k_ref, v_ref, pek_ref, pev_ref, w1k_ref, w1v_ref, w2k_ref, w2v_ref,
                     kc_ref, vc_ref):
    nc = k_ref.shape[2]

    def run(a, pe_ref, w1_ref, w2_ref):
        pa = _dot((a + pe_ref[0:1, :]).astype(BF16), w1_ref[0])
        pb = _dot((a + pe_ref[1:2, :]).astype(BF16), w1_ref[1])
        pb = jnp.concatenate([pb[1:], jnp.zeros((1, HEAD_DIM), F32)], axis=0)
        hid = _gelu_tanh(pa + pb)
        return _dot(hid.astype(BF16), w2_ref[...])

    zeros = jnp.zeros((nc, HEAD_DIM), F32)
    kc = run(k_ref[0, 0], pek_ref, w1k_ref, w2k_ref)
    kc_ref[0, 0] = jnp.concatenate([kc, zeros], axis=1).astype(BF16)
    vc = run(v_ref[0, 0], pev_ref, w1v_ref, w2v_ref)
    vc_ref[0, 0] = jnp.concatenate([vc, zeros], axis=1).astype(BF16)


def _compress(kc4, vc4, pek, pev, w1k, w1v, w2k, w2v):
    b, g, nc, width = kc4.shape
    half = CMP_STRIDE * HEAD_DIM
    blk = pl.BlockSpec((1, 1, nc, width), lambda bi, gi: (bi, gi, 0, 0))
    c2 = lambda bi, gi: (0, 0)
    c3 = lambda bi, gi: (0, 0, 0)
    return pl.pallas_call(
        _compress_kernel,
        grid=(b, g),
        in_specs=[blk, blk,
                  pl.BlockSpec((2, half), c2), pl.BlockSpec((2, half), c2),
                  pl.BlockSpec((2, half, HEAD_DIM), c3), pl.BlockSpec((2, half, HEAD_DIM), c3),
                  pl.BlockSpec((HEAD_DIM, HEAD_DIM), c2), pl.BlockSpec((HEAD_DIM, HEAD_DIM), c2)],
        out_specs=(pl.BlockSpec((1, 1, nc, LANES), lambda bi, gi: (bi, gi, 0, 0)),
                   pl.BlockSpec((1, 1, nc, LANES), lambda bi, gi: (bi, gi, 0, 0))),
        out_shape=(jax.ShapeDtypeStruct((b, g, nc, LANES), BF16),
                   jax.ShapeDtypeStruct((b, g, nc, LANES), BF16)),
        compiler_params=_cparams(2),
        name="compress",
    )(kc4, vc4, pek, pev, w1k, w1v, w2k, w2v)


def _attn_kernel(q_ref, kc_ref, vc_ref, ks_ref, vs_ref, kw_ref, vw_ref, gl_ref, ovl_ref, o_ref):
    tq = q_ref.shape[2]
    nc = kc_ref.shape[2]
    s_len = ks_ref.shape[2]
    rows = GQA_R * tq
    q0 = pl.program_id(2) * tq

    q4 = q_ref[0]
    t_col = q0 + lax.broadcasted_iota(jnp.int32, (tq, 1), 0)
    t_rows = jnp.concatenate([t_col] * GQA_R, axis=0)

    def head(a, r):
        return a[r * tq:(r + 1) * tq]

    def pv_heads(pb, v):
        return [_dot(head(pb, r), v) for r in range(GQA_R)]

    def weights(s, m):
        return jnp.exp2((s - m).astype(BF16))

    q_plain = q4.reshape(rows, LANES)
    sc = _dot_nt(q_plain, kc_ref[0, 0])
    cmp_end = lax.broadcasted_iota(jnp.int32, (1, nc), 1) * CMP_STRIDE + (CMP_BLOCK - 1)
    sc = jnp.where(cmp_end <= t_rows, sc, -jnp.inf)
    m = jnp.max(sc, axis=-1, keepdims=True)
    m = jnp.where(m == -jnp.inf, 0.0, m)
    e = jnp.exp2(sc - m)
    p_cmp = e / jnp.maximum(jnp.sum(e, axis=-1, keepdims=True), 1e-30)
    o_cmp = pv_heads(p_cmp.astype(BF16), vc_ref[0, 0])

    p_sum = p_cmp[0:tq] + p_cmp[tq:2 * tq] + p_cmp[2 * tq:3 * tq] + p_cmp[3 * tq:]
    p_hi = p_sum.astype(BF16)
    rem = p_sum - p_hi.astype(F32)
    p_mid = rem.astype(BF16)
    p_lo = (rem - p_mid.astype(F32)).astype(BF16)
    ovl_t = ovl_ref[...]
    imp = _dot_nt(ovl_t, p_hi) + _dot_nt(ovl_t, p_mid) + _dot_nt(ovl_t, p_lo)
    blk = lax.broadcasted_iota(jnp.int32, (N_BLK, tq), 0)
    cur = (q0 + lax.broadcasted_iota(jnp.int32, (1, tq), 1)) // SEL_BLOCK
    causal = blk <= cur
    forced = (blk == 0) | (causal & (blk > cur - N_FORCED_LOCAL))
    val = jnp.where(forced, FORCE_SCORE, jnp.where(causal, imp, NOT_CAUSAL_SCORE))

    n_tiles = N_BLK // SUBLANES
    row8 = lax.broadcasted_iota(jnp.int32, (SUBLANES, tq), 0)
    tiles = [val[a * SUBLANES:(a + 1) * SUBLANES] for a in range(n_tiles)]
    ranks = [jnp.zeros((SUBLANES, tq), F32) for _ in range(n_tiles)]
    for mp in range(N_BLK):
        b = jnp.broadcast_to(val[mp:mp + 1], (SUBLANES, tq))
        for a in range(n_tiles):
            if mp < a * SUBLANES:
                inc = jnp.where(b >= tiles[a], 1.0, 0.0)
            elif mp >= (a + 1) * SUBLANES:
                inc = jnp.where(b > tiles[a], 1.0, 0.0)
            else:
                inc = jnp.where(row8 > mp - a * SUBLANES, jnp.where(b >= tiles[a], 1.0, 0.0),
                                jnp.where(b > tiles[a], 1.0, 0.0))
            ranks[a] = ranks[a] + inc
    rank = jnp.concatenate(ranks, axis=0)
    bias_t = jnp.where(rank < float(SEL_TOPK), 0.0, BLOCK_MASK_BIAS)
    bias = jnp.concatenate([jnp.zeros((HEAD_DIM, tq), F32), bias_t], axis=0).T
    q_aug = (q4 + bias.astype(BF16)[None]).reshape(rows, LANES)

    tk_wide = min(TK_SEL, s_len)

    def sel_step(j, carry, base, tk, diagonal):
        m_i, accs = carry
        k0 = pl.multiple_of(base + j * tk, tk)
        s = _dot_nt(q_aug, ks_ref[0, 0, pl.ds(k0, tk), :])
        if diagonal:
            kpos = k0 + lax.broadcasted_iota(jnp.int32, (1, tk), 1)
            s = jnp.where(kpos <= t_rows, s, -jnp.inf)
        m_new = jnp.maximum(m_i, jnp.max(s, axis=-1, keepdims=True))
        alpha = jnp.exp2(m_i - m_new)
        outs = pv_heads(weights(s, m_new), vs_ref[0, 0, pl.ds(k0, tk), :])
        return m_new, tuple(head(alpha, r) * accs[r] + outs[r] for r in range(GQA_R))

    tk_diag = min(TK_SEL_DIAG, s_len)
    n_wide = q0 // tk_wide
    diag0 = (q0 // tk_diag) * tk_diag
    n_mid = (diag0 - n_wide * tk_wide) // tk_diag
    zero_acc = jnp.zeros((tq, LANES), F32)
    carry = (jnp.full((rows, 1), -jnp.inf, F32), (zero_acc,) * GQA_R)
    carry = lax.fori_loop(
        0, n_wide, functools.partial(sel_step, base=0, tk=tk_wide, diagonal=False), carry)
    carry = lax.fori_loop(
        0, n_mid,
        functools.partial(sel_step, base=n_wide * tk_wide, tk=tk_diag, diagonal=False), carry)
    _, o_sel = sel_step(0, carry, base=diag0, tk=tk_diag, diagonal=True)

    wlen = min(WINDOW + tq, s_len)
    w0 = pl.multiple_of(jnp.maximum(q0 + tq - wlen, 0), tq)
    sw = _dot_nt(q_plain, kw_ref[0, 0, pl.ds(w0, wlen), :])
    kp = w0 + lax.broadcasted_iota(jnp.int32, (1, wlen), 1)
    age = lax.bitcast_convert_type(t_rows - kp[:, :tq], jnp.uint32)
    sw = jnp.concatenate([jnp.where(age < WINDOW, sw[:, :tq], -jnp.inf),
                          jnp.where(kp[:, tq:] <= t_rows, sw[:, tq:], -jnp.inf)], axis=1)
    o_win = pv_heads(weights(sw, jnp.max(sw, axis=-1, keepdims=True)),
                     vw_ref[0, 0, pl.ds(w0, wlen), :])

    sg = jax.nn.sigmoid(gl_ref[0, 0])
    rowsum = lambda a: a[:, ROWSUM_LANE:ROWSUM_LANE + 1]
    heads = []
    for r in range(GQA_R):
        g_cmp, g_sel, g_win = (sg[:, 3 * r + j:3 * r + j + 1] for j in range(3))
        o = (g_cmp * o_cmp[r] + (g_sel / rowsum(o_sel[r])) * o_sel[r]
             + (g_win / rowsum(o_win[r])) * o_win[r])
        heads.append(o[:, :HEAD_DIM])
    o_ref[0] = jnp.concatenate(heads, axis=1).astype(BF16)


def _attention(q, kc_aug, vc_p, ks_aug, vs_p, kw_aug, vw_p, gl, ovl):
    b, _, s, _ = q.shape
    nc = kc_aug.shape[2]
    tq = min(TQ, s)
    kv = lambda n: pl.BlockSpec((1, 1, n, LANES), lambda bi, gi, i: (bi, gi, 0, 0))
    return pl.pallas_call(
        _attn_kernel,
        grid=(b, N_KV, s // tq),
        in_specs=[pl.BlockSpec((1, GQA_R, tq, LANES), lambda bi, gi, i: (bi, gi, i, 0)),
                  kv(nc), kv(nc), kv(s), kv(s), kv(s), kv(s),
                  pl.BlockSpec((1, 1, tq, LANES), lambda bi, gi, i: (bi, gi, i, 0)),
                  pl.BlockSpec((N_BLK, nc), lambda bi, gi, i: (0, 0))],
        out_specs=pl.BlockSpec((1, tq, GQA_R * HEAD_DIM), lambda bi, gi, i: (bi, i, gi)),
        out_shape=jax.ShapeDtypeStruct((b, s, NSA_WIDTH), BF16),
        compiler_params=_cparams(3),
        name="nsa_attention",
    )(q, kc_aug, vc_p, ks_aug, vs_p, kw_aug, vw_p, gl, ovl)


def _merge_kernel(x_ref, mod_ref, gpre_ref, gpost_ref, on_ref, u_ref, uprev_ref, wbrg_ref,
                  wpool_ref, pscale_ref, wbn_ref, wbp_ref, wout_ref, o_ref):
    tm = x_ref.shape[1]
    i = pl.program_id(1)
    x = x_ref[0]
    h = (_rms(x) * gpre_ref[...]) * (1.0 + mod_ref[0, 1:2, :]) + mod_ref[0, 0:1, :]
    br = jax.nn.sigmoid(_dot(h.astype(BF16), wbrg_ref[...]))

    u = u_ref[0]
    prev = jnp.where(i > 0, uprev_ref[0], 0.0)
    ext = jnp.concatenate([prev, u], axis=0)
    tpos = i * tm + lax.broadcasted_iota(jnp.int32, (tm, 1), 0)
    mixed = []
    for gidx, w in enumerate(POOL_SIZES):
        acc = ext[:, gidx * POOL_GROUP:(gidx + 1) * POOL_GROUP]
        span = 1
        while span < w:
            acc = acc + pltpu.roll(acc, span, 0)
            span *= 2
        cnt = jnp.minimum(tpos + 1, w).astype(F32)
        pooled = acc[POOL_HALO:] / cnt - u[:, gidx * POOL_GROUP:(gidx + 1) * POOL_GROUP]
        mixed.append(_dot(pooled.astype(BF16), wpool_ref[gidx]))
    o_pool = jnp.concatenate(mixed, axis=1) * pscale_ref[...]

    merged = (br[:, :D_MODEL] * _dot(on_ref[0], wbn_ref[...])
              + br[:, D_MODEL:] * _dot(o_pool.astype(BF16), wbp_ref[...]))
    y = _dot(merged.astype(BF16), wout_ref[...])
    o_ref[0] = x + mod_ref[0, 2:3, :] * (_rms(y) * gpost_ref[...])


def _merge(x, mod, g_pre, g_post, o_nsa, u, w_brg, w_pool, pool_scale, w_br_nsa, w_br_pool, w_out):
    b, s, d = x.shape
    tm = min(TM_PROJ, s)
    halo_per_tile = tm // POOL_HALO
    tile = lambda w: pl.BlockSpec((1, tm, w), lambda bi, i: (bi, i, 0))
    c2 = lambda shape: pl.BlockSpec(shape, lambda bi, i: (0, 0))
    return pl.pallas_call(
        _merge_kernel,
        grid=(b, s // tm),
        in_specs=[tile(d),
                  pl.BlockSpec((1, N_MOD, d), lambda bi, i: (bi, 0, 0)),
                  c2((1, d)), c2((1, d)),
                  tile(NSA_WIDTH), tile(POOL_WIDTH),
                  pl.BlockSpec((1, POOL_HALO, POOL_WIDTH),
                               lambda bi, i: (bi, jnp.maximum(i * halo_per_tile - 1, 0), 0)),
                  c2((d, 2 * d)),
                  pl.BlockSpec((len(POOL_SIZES), POOL_GROUP, POOL_GROUP), lambda bi, i: (0, 0, 0)),
                  c2((1, POOL_WIDTH)),
                  c2((NSA_WIDTH, d)), c2((POOL_WIDTH, d)), c2((d, d))],
        out_specs=tile(d),
        out_shape=jax.ShapeDtypeStruct((b, s, d), F32),
        compiler_params=_cparams(2),
        name="merge",
    )(x, mod, g_pre, g_post, o_nsa, u, u, w_brg, w_pool, pool_scale, w_br_nsa, w_br_pool, w_out)


def _mlp_kernel(x_ref, mod_ref, gpre_ref, gpost_ref, w1_ref, w2_ref, o_ref):
    x = x_ref[0]
    h = ((_rms(x) * gpre_ref[...]) * (1.0 + mod_ref[0, 4:5, :]) + mod_ref[0, 3:4, :]).astype(BF16)
    y = jnp.zeros(x.shape, F32)
    for c0 in range(0, D_FF, FF_CHUNK):
        a = jnp.maximum(_dot(h, w1_ref[:, c0:c0 + FF_CHUNK]), 0.0)
        y = y + _dot((a * a).astype(BF16), w2_ref[c0:c0 + FF_CHUNK, :])
    o_ref[0] = x + mod_ref[0, 5:6, :] * (_rms(y) * gpost_ref[...])


def _mlp(x, mod, g_pre, g_post, w_fc1, w_fc2):
    b, s, d = x.shape
    tm = min(TM_PROJ, s)
    tile = pl.BlockSpec((1, tm, d), lambda bi, i: (bi, i, 0))
    resident = lambda shape: pl.BlockSpec(shape, lambda bi, i: (0, 0), pipeline_mode=pl.Buffered(1))
    return pl.pallas_call(
        _mlp_kernel,
        grid=(b, s // tm),
        in_specs=[tile,
                  pl.BlockSpec((1, N_MOD, d), lambda bi, i: (bi, 0, 0)),
                  pl.BlockSpec((1, d), lambda bi, i: (0, 0)),
                  pl.BlockSpec((1, d), lambda bi, i: (0, 0)),
                  resident((d, D_FF)), resident((D_FF, d))],
        out_specs=tile,
        out_shape=jax.ShapeDtypeStruct((b, s, d), F32),
        compiler_params=_cparams(2),
        name="mlp",
    )(x, mod, g_pre, g_post, w_fc1, w_fc2)


def _rope_tables(s):
    half = HEAD_DIM // 2
    inv = ROPE_THETA ** (-jnp.arange(half, dtype=F32) / half)
    ang = jnp.arange(s, dtype=F32)[:, None] * inv[None, :]
    cos = jnp.cos(ang)
    sin = jnp.sin(ang)
    reps = LANES // HEAD_DIM
    cos_t = jnp.tile(jnp.concatenate([cos, cos], axis=1), (1, reps))
    sin_t = jnp.tile(jnp.concatenate([-sin, sin], axis=1), (1, reps))
    return cos_t, sin_t


def _overlap_table(nc, n_sel):
    cmp_start = np.arange(nc) * CMP_STRIDE
    sel_start = np.arange(n_sel) * SEL_BLOCK
    ovl = ((cmp_start[None, :] < sel_start[:, None] + SEL_BLOCK)
           & (cmp_start[None, :] + CMP_BLOCK > sel_start[:, None])).astype(np.float32)
    out = np.zeros((N_BLK, nc), np.float32)
    out[:n_sel] = ovl
    return jnp.asarray(out, dtype=BF16)


def _pack_in_weights(w_in):
    q, kc, vc, ks, vs, kw, vw, gate, u, brg = jnp.split(w_in, IN_SPLITS, axis=-1)
    d = w_in.shape[0]
    gate = gate.reshape(d, N_KV, GQA_R * 3)
    gate = jnp.pad(gate, ((0, 0), (0, 0), (0, LANES - GQA_R * 3))).reshape(d, N_KV * LANES)
    cat = jnp.concatenate([q * (HEAD_DIM ** -0.5 * LOG2E), kc, ks, kw, vc, vs, vw, gate, u], axis=1)
    return cat.astype(BF16), brg.astype(BF16)


def kernel(x, c, w_ada, b_ada, g_pre_mix, g_post_mix, w_in, cmp_pe_k, cmp_w1_k, cmp_w2_k,
           cmp_pe_v, cmp_w1_v, cmp_w2_v, w_pool, pool_scale, w_br_nsa, w_br_pool, w_out,
           g_pre_mlp, g_post_mlp, w_fc1, w_fc2):
    b, s, d = x.shape
    depth = w_ada.shape[0]
    n_sel = s // SEL_BLOCK
    nc = s // CMP_STRIDE
    assert d == D_MODEL and s % TM_PROJ == 0 and s >= WINDOW + TQ
    assert SEL_TOPK <= n_sel <= N_BLK
    cos_t, sin_t = _rope_tables(s)
    ovl = _overlap_table(nc, n_sel)
    half = CMP_STRIDE * HEAD_DIM

    for l in range(depth):
        mod = _modulation(c, w_ada[l], b_ada[l]).reshape(b, N_MOD, d)
        w_cat, w_brg = _pack_in_weights(w_in[l])
        q, kc, vc, ks_aug, kw_aug, vs_p, vw_p, gl, u = _inproj(
            x, mod, g_pre_mix[l].reshape(1, d), w_cat, cos_t, sin_t)
        kc_aug, vc_p = _compress(
            kc.reshape(b, N_KV, nc, half), vc.reshape(b, N_KV, nc, half),
            cmp_pe_k[l].reshape(2, half), cmp_pe_v[l].reshape(2, half),
            cmp_w1_k[l].reshape(2, half, HEAD_DIM).astype(BF16),
            cmp_w1_v[l].reshape(2, half, HEAD_DIM).astype(BF16),
            cmp_w2_k[l].astype(BF16), cmp_w2_v[l].astype(BF16))
        o_nsa = _attention(q, kc_aug, vc_p, ks_aug, vs_p, kw_aug, vw_p, gl, ovl)
        x = _merge(x, mod, g_pre_mix[l].reshape(1, d), g_post_mix[l].reshape(1, d), o_nsa, u,
                   w_brg, w_pool[l].astype(BF16), pool_scale[l].reshape(1, POOL_WIDTH),
                   w_br_nsa[l].astype(BF16), w_br_pool[l].astype(BF16), w_out[l].astype(BF16))
        x = _mlp(x, mod, g_pre_mlp[l].reshape(1, d), g_post_mlp[l].reshape(1, d),
                 w_fc1[l].astype(BF16), w_fc2[l].astype(BF16))
    return x
```

```python
import functools

import numpy as np
import jax
import jax.numpy as jnp
from jax import lax
from jax.experimental import pallas as pl
from jax.experimental.pallas import tpu as pltpu

F32 = jnp.float32
BF16 = jnp.bfloat16

D_MODEL = 1024
N_HEADS = 8
N_KV = 2
HEAD_DIM = 64
GQA_R = N_HEADS // N_KV
NSA_WIDTH = N_HEADS * HEAD_DIM
KV_WIDTH = N_KV * HEAD_DIM
CMP_BLOCK = 32
CMP_STRIDE = 16
SEL_BLOCK = 64
SEL_TOPK = 16
N_FORCED_LOCAL = 2
WINDOW = 512
FORCE_SCORE = 1e9
POOL_SIZES = (2, 4, 8, 16)
POOL_GROUP = 128
POOL_WIDTH = POOL_GROUP * len(POOL_SIZES)
D_FF = 4 * D_MODEL
ROPE_THETA = 10000.0
EPS = 1e-6
N_MOD = 6
IN_SIZES = (NSA_WIDTH, KV_WIDTH, KV_WIDTH, KV_WIDTH, KV_WIDTH, KV_WIDTH, KV_WIDTH,
            3 * N_HEADS, POOL_WIDTH, 2 * D_MODEL)
IN_SPLITS = [int(v) for v in np.cumsum(IN_SIZES)[:-1]]

LANES = 128
SUBLANES = 8
N_BLK = LANES - HEAD_DIM
ROWSUM_LANE = HEAD_DIM
LOG2E = 1.4426950408889634
POOL_HALO = 16
BLOCK_MASK_BIAS = -float(2 ** 30)
NOT_CAUSAL_SCORE = -1e30

C_Q = 0
C_KC = C_Q + NSA_WIDTH
C_KS = C_KC + KV_WIDTH
C_KW = C_KS + KV_WIDTH
C_VC = C_KW + KV_WIDTH
C_VS = C_VC + KV_WIDTH
C_VW = C_VS + KV_WIDTH
C_GATE = C_VW + KV_WIDTH
C_U = C_GATE + N_KV * LANES
IN_COLS = C_U + POOL_WIDTH

TM_PROJ = 512
TQ = 256
TK_SEL = 1024
TK_SEL_DIAG = 512
FF_CHUNK = 1024
VMEM_LIMIT = 56 * 1024 * 1024


def _cparams(n_axes):
    return pltpu.CompilerParams(dimension_semantics=("arbitrary",) * n_axes,
                                vmem_limit_bytes=VMEM_LIMIT)


def _dot(a, b):
    return jnp.dot(a, b, preferred_element_type=F32)


def _dot_nt(a, b):
    return lax.dot_general(a, b, (((1,), (1,)), ((), ())), preferred_element_type=F32)


def _rms(x):
    return x * lax.rsqrt(jnp.mean(x * x, axis=-1, keepdims=True) + EPS)


def _mod_kernel(c_ref, w_ref, b_ref, o_ref):
    c = c_ref[...]
    a = c * jax.nn.sigmoid(c)
    o_ref[...] = jnp.dot(a, w_ref[...], preferred_element_type=F32,
                         precision=lax.Precision.HIGHEST) + b_ref[...]


def _modulation(c, w_ada, b_ada, layer):
    b, d = c.shape
    n = w_ada.shape[2]
    tn = 1024
    return pl.pallas_call(
        _mod_kernel,
        grid=(n // tn,),
        in_specs=[pl.BlockSpec((b, d), lambda j: (0, 0)),
                  pl.BlockSpec((None, d, tn), lambda j: (layer, 0, j)),
                  pl.BlockSpec((None, 1, tn), lambda j: (layer, 0, j))],
        out_specs=pl.BlockSpec((b, tn), lambda j: (0, j)),
        out_shape=jax.ShapeDtypeStruct((b, n), F32),
        compiler_params=_cparams(1),
        name="modulation",
    )(c, w_ada, b_ada.reshape(-1, 1, n))


def _inproj_kernel(x_ref, mod_ref, g_ref, w_ref, cos_ref, sin_ref,
                   q_ref, kc_ref, vc_ref, ks_ref, kw_ref, vs_ref, vw_ref, gl_ref, u_ref):
    tm = x_ref.shape[1]
    h = (_rms(x_ref[0]) * g_ref[...]) * (1.0 + mod_ref[0, 1:2, :]) + mod_ref[0, 0:1, :]
    p = _dot(h.astype(BF16), w_ref[...])

    lane = lax.broadcasted_iota(jnp.int32, (tm, LANES), 1)
    lo = lane < HEAD_DIM
    first_half = (lane % HEAD_DIM) < (HEAD_DIM // 2)
    cos = cos_ref[...]
    sin = sin_ref[...]

    def rope(v):
        rot = jnp.where(first_half, pltpu.roll(v, LANES - HEAD_DIM // 2, 1),
                        pltpu.roll(v, HEAD_DIM // 2, 1))
        return v * cos + rot * sin

    def swap(v):
        return pltpu.roll(v, HEAD_DIM, 1)

    def slab(c0):
        return p[:, c0:c0 + LANES]

    for j in range(N_HEADS // 2):
        v = rope(slab(C_Q + j * LANES))
        q_ref[0, 2 * j] = jnp.where(lo, v, 0.0).astype(BF16)
        q_ref[0, 2 * j + 1] = jnp.where(lo, swap(v), 0.0).astype(BF16)

    kc_ref[0] = rope(slab(C_KC))
    vc_ref[0] = slab(C_VC)

    pos = pl.program_id(1) * tm + lax.broadcasted_iota(jnp.int32, (tm, LANES), 0)
    onehot = jnp.where(lane - HEAD_DIM == pos // SEL_BLOCK, 1.0, 0.0)
    v = rope(slab(C_KS))
    ks_ref[0, 0] = jnp.where(lo, v, onehot).astype(BF16)
    ks_ref[0, 1] = jnp.where(lo, swap(v), onehot).astype(BF16)
    v = rope(slab(C_KW))
    kw_ref[0, 0] = jnp.where(lo, v, 0.0).astype(BF16)
    kw_ref[0, 1] = jnp.where(lo, swap(v), 0.0).astype(BF16)

    ones_col = jnp.where(lane == ROWSUM_LANE, 1.0, 0.0)
    for c0, ref in ((C_VS, vs_ref), (C_VW, vw_ref)):
        v = slab(c0)
        ref[0, 0] = jnp.where(lo, v, ones_col).astype(BF16)
        ref[0, 1] = jnp.where(lo, swap(v), ones_col).astype(BF16)

    for g in range(N_KV):
        gl_ref[0, g] = slab(C_GATE + g * LANES)
    u_ref[0] = p[:, C_U:C_U + POOL_WIDTH]


def _inproj(x, mod, g_pre, w_cat, cos_t, sin_t):
    b, s, d = x.shape
    tm = min(TM_PROJ, s)
    nt = s // tm
    const = lambda bi, i: (0, 0)
    out_shape = (
        jax.ShapeDtypeStruct((b, N_HEADS, s, LANES), BF16),
        jax.ShapeDtypeStruct((b, s, KV_WIDTH), F32),
        jax.ShapeDtypeStruct((b, s, KV_WIDTH), F32),
        jax.ShapeDtypeStruct((b, N_KV, s, LANES), BF16),
        jax.ShapeDtypeStruct((b, N_KV, s, LANES), BF16),
        jax.ShapeDtypeStruct((b, N_KV, s, LANES), BF16),
        jax.ShapeDtypeStruct((b, N_KV, s, LANES), BF16),
        jax.ShapeDtypeStruct((b, N_KV, s, LANES), F32),
        jax.ShapeDtypeStruct((b, s, POOL_WIDTH), F32),
    )
    kv4 = lambda w: pl.BlockSpec((1, N_KV, tm, w), lambda bi, i: (bi, 0, i, 0))
    tok = lambda w: pl.BlockSpec((1, tm, w), lambda bi, i: (bi, i, 0))
    out_specs = (
        pl.BlockSpec((1, N_HEADS, tm, LANES), lambda bi, i: (bi, 0, i, 0)),
        tok(KV_WIDTH), tok(KV_WIDTH), kv4(LANES), kv4(LANES), kv4(LANES), kv4(LANES), kv4(LANES),
        tok(POOL_WIDTH),
    )
    return pl.pallas_call(
        _inproj_kernel,
        grid=(b, nt),
        in_specs=[pl.BlockSpec((1, tm, d), lambda bi, i: (bi, i, 0)),
                  pl.BlockSpec((1, N_MOD, d), lambda bi, i: (bi, 0, 0)),
                  pl.BlockSpec((1, d), const),
                  pl.BlockSpec((d, IN_COLS), const),
                  pl.BlockSpec((tm, LANES), lambda bi, i: (i, 0)),
                  pl.BlockSpec((tm, LANES), lambda bi, i: (i, 0))],
        out_specs=out_specs,
        out_shape=out_shape,
        compiler_params=_cparams(2),
        name="inproj",
    )(x, mod, g_pre, w_cat, cos_t, sin_t)


def _gelu_tanh(x):
    return 0.5 * x * (1.0 + jnp.tanh(np.sqrt(2.0 / np.pi) * (x + 0.044715 * (x * x * x))))


def _compress_kernel(k_ref, v_ref, pek_ref, pev_ref, w1k_ref, w1v_ref, w2k_ref, w2v_ref,
                     kc_ref, vc_ref):
    nc = kc_ref.shape[2]
    lane = lax.broadcasted_iota(jnp.int32, (nc, LANES), 1)
    lo = lane < HEAD_DIM

    def run(x_ref, pe_ref, w1_ref, w2_ref):
        chunks = jnp.concatenate(
            [x_ref[0, pl.ds(l, nc, stride=CMP_STRIDE), :] for l in range(CMP_STRIDE)], axis=1)
        pa = _dot((chunks + pe_ref[0:1, :]).astype(BF16), w1_ref[0])
        pb = _dot((chunks + pe_ref[1:2, :]).astype(BF16), w1_ref[1])
        pb = jnp.concatenate([pb[1:], jnp.zeros((1, LANES), F32)], axis=0)
        hid = _gelu_tanh(pa + pb)
        out = _dot(hid.astype(BF16), w2_ref[...])
        return (jnp.where(lo, out, 0.0).astype(BF16),
                jnp.where(lo, pltpu.roll(out, HEAD_DIM, 1), 0.0).astype(BF16))

    kc_ref[0, 0], kc_ref[0, 1] = run(k_ref, pek_ref, w1k_ref, w2k_ref)
    vc_ref[0, 0], vc_ref[0, 1] = run(v_ref, pev_ref, w1v_ref, w2v_ref)


def _compress_weights(pe, w1, w2):
    eye = jnp.eye(N_KV, dtype=F32)
    pe2 = jnp.tile(pe.reshape(2, CMP_STRIDE, 1, HEAD_DIM), (1, 1, N_KV, 1))
    pe2 = pe2.reshape(2, CMP_STRIDE * KV_WIDTH)
    w1r = w1.reshape(2, CMP_STRIDE, HEAD_DIM, HEAD_DIM)
    w1b = jnp.einsum("hlde,gk->hlgdke", w1r, eye).reshape(2, CMP_STRIDE * KV_WIDTH, KV_WIDTH)
    w2b = jnp.einsum("de,gk->gdke", w2, eye).reshape(KV_WIDTH, KV_WIDTH)
    return pe2, w1b.astype(BF16), w2b.astype(BF16)


def _compress(kc, vc, pek, pev, w1k, w1v, w2k, w2v):
    b, s, width = kc.shape
    nc = s // CMP_STRIDE
    wide = CMP_STRIDE * width
    blk = pl.BlockSpec((1, s, width), lambda bi: (bi, 0, 0))
    c2 = lambda bi: (0, 0)
    c3 = lambda bi: (0, 0, 0)
    out = pl.BlockSpec((1, N_KV, nc, LANES), lambda bi: (bi, 0, 0, 0))
    return pl.pallas_call(
        _compress_kernel,
        grid=(b,),
        in_specs=[blk, blk,
                  pl.BlockSpec((2, wide), c2), pl.BlockSpec((2, wide), c2),
                  pl.BlockSpec((2, wide, width), c3), pl.BlockSpec((2, wide, width), c3),
                  pl.BlockSpec((width, width), c2), pl.BlockSpec((width, width), c2)],
        out_specs=(out, out),
        out_shape=(jax.ShapeDtypeStruct((b, N_KV, nc, LANES), BF16),
                   jax.ShapeDtypeStruct((b, N_KV, nc, LANES), BF16)),
        compiler_params=_cparams(1),
        name="compress",
    )(kc, vc, pek, pev, w1k, w1v, w2k, w2v)


def _attn_kernel(q_ref, kc_ref, vc_ref, ks_ref, vs_ref, kw_ref, vw_ref, gl_ref, ovl_ref, o_ref):
    tq = q_ref.shape[2]
    nc = kc_ref.shape[2]
    s_len = ks_ref.shape[2]
    rows = GQA_R * tq
    q0 = pl.program_id(2) * tq

    q4 = q_ref[0]
    t_col = q0 + lax.broadcasted_iota(jnp.int32, (tq, 1), 0)
    t_rows = jnp.concatenate([t_col] * GQA_R, axis=0)

    def head(a, r):
        return a[r * tq:(r + 1) * tq]

    def pv_heads(pb, v):
        return [_dot(head(pb, r), v) for r in range(GQA_R)]

    def weights(s, m):
        return jnp.exp2((s - m).astype(BF16))

    q_plain = q4.reshape(rows, LANES)
    sc = _dot_nt(q_plain, kc_ref[0, 0])
    cmp_end = lax.broadcasted_iota(jnp.int32, (1, nc), 1) * CMP_STRIDE + (CMP_BLOCK - 1)
    sc = jnp.where(cmp_end <= t_rows, sc, -jnp.inf)
    m = jnp.max(sc, axis=-1, keepdims=True)
    m = jnp.where(m == -jnp.inf, 0.0, m)
    e = jnp.exp2(sc - m)
    p_cmp = e / jnp.maximum(jnp.sum(e, axis=-1, keepdims=True), 1e-30)
    o_cmp = pv_heads(p_cmp.astype(BF16), vc_ref[0, 0])

    p_sum = p_cmp[0:tq] + p_cmp[tq:2 * tq] + p_cmp[2 * tq:3 * tq] + p_cmp[3 * tq:]
    p_hi = p_sum.astype(BF16)
    rem = p_sum - p_hi.astype(F32)
    p_mid = rem.astype(BF16)
    p_lo = (rem - p_mid.astype(F32)).astype(BF16)
    ovl_t = ovl_ref[...]
    imp = _dot_nt(ovl_t, p_hi) + _dot_nt(ovl_t, p_mid) + _dot_nt(ovl_t, p_lo)
    blk = lax.broadcasted_iota(jnp.int32, (N_BLK, tq), 0)
    cur = (q0 + lax.broadcasted_iota(jnp.int32, (1, tq), 1)) // SEL_BLOCK
    causal = blk <= cur
    forced = (blk == 0) | (causal & (blk > cur - N_FORCED_LOCAL))
    val = jnp.where(forced, FORCE_SCORE, jnp.where(causal, imp, NOT_CAUSAL_SCORE))

    n_tiles = N_BLK // SUBLANES
    row8 = lax.broadcasted_iota(jnp.int32, (SUBLANES, tq), 0)
    tiles = [val[a * SUBLANES:(a + 1) * SUBLANES] for a in range(n_tiles)]
    ranks = [jnp.zeros((SUBLANES, tq), F32) for _ in range(n_tiles)]
    for mp in range(N_BLK):
        b = jnp.broadcast_to(val[mp:mp + 1], (SUBLANES, tq))
        for a in range(n_tiles):
            if mp < a * SUBLANES:
                inc = jnp.where(b >= tiles[a], 1.0, 0.0)
            elif mp >= (a + 1) * SUBLANES:
                inc = jnp.where(b > tiles[a], 1.0, 0.0)
            else:
                inc = jnp.where(row8 > mp - a * SUBLANES, jnp.where(b >= tiles[a], 1.0, 0.0),
                                jnp.where(b > tiles[a], 1.0, 0.0))
            ranks[a] = ranks[a] + inc
    rank = jnp.concatenate(ranks, axis=0)
    bias_t = jnp.where(rank < float(SEL_TOPK), 0.0, BLOCK_MASK_BIAS)
    bias = jnp.concatenate([jnp.zeros((HEAD_DIM, tq), F32), bias_t], axis=0).T
    q_aug = (q4 + bias.astype(BF16)[None]).reshape(rows, LANES)

    tk_wide = min(TK_SEL, s_len)

    def sel_step(j, carry, base, tk, diagonal):
        m_i, accs = carry
        k0 = pl.multiple_of(base + j * tk, tk)
        s = _dot_nt(q_aug, ks_ref[0, 0, pl.ds(k0, tk), :])
        if diagonal:
            kpos = k0 + lax.broadcasted_iota(jnp.int32, (1, tk), 1)
            s = jnp.where(kpos <= t_rows, s, -jnp.inf)
        m_new = jnp.maximum(m_i, jnp.max(s, axis=-1, keepdims=True))
        alpha = jnp.exp2(m_i - m_new)
        outs = pv_heads(weights(s, m_new), vs_ref[0, 0, pl.ds(k0, tk), :])
        return m_new, tuple(head(alpha, r) * accs[r] + outs[r] for r in range(GQA_R))

    tk_diag = min(TK_SEL_DIAG, s_len)
    n_wide = q0 // tk_wide
    diag0 = (q0 // tk_diag) * tk_diag
    n_mid = (diag0 - n_wide * tk_wide) // tk_diag
    zero_acc = jnp.zeros((tq, LANES), F32)
    carry = (jnp.full((rows, 1), -jnp.inf, F32), (zero_acc,) * GQA_R)
    carry = lax.fori_loop(
        0, n_wide, functools.partial(sel_step, base=0, tk=tk_wide, diagonal=False), carry)
    carry = lax.fori_loop(
        0, n_mid,
        functools.partial(sel_step, base=n_wide * tk_wide, tk=tk_diag, diagonal=False), carry)
    _, o_sel = sel_step(0, carry, base=diag0, tk=tk_diag, diagonal=True)

    wlen = min(WINDOW + tq, s_len)
    w0 = pl.multiple_of(jnp.maximum(q0 + tq - wlen, 0), tq)
    sw = _dot_nt(q_plain, kw_ref[0, 0, pl.ds(w0, wlen), :])
    kp = w0 + lax.broadcasted_iota(jnp.int32, (1, wlen), 1)
    age = lax.bitcast_convert_type(t_rows - kp[:, :tq], jnp.uint32)
    sw = jnp.concatenate([jnp.where(age < WINDOW, sw[:, :tq], -jnp.inf),
                          jnp.where(kp[:, tq:] <= t_rows, sw[:, tq:], -jnp.inf)], axis=1)
    o_win = pv_heads(weights(sw, jnp.max(sw, axis=-1, keepdims=True)),
                     vw_ref[0, 0, pl.ds(w0, wlen), :])

    sg = jax.nn.sigmoid(gl_ref[0, 0])
    rowsum = lambda a: a[:, ROWSUM_LANE:ROWSUM_LANE + 1]
    heads = []
    for r in range(GQA_R):
        g_cmp, g_sel, g_win = (sg[:, 3 * r + j:3 * r + j + 1] for j in range(3))
        o = (g_cmp * o_cmp[r] + (g_sel / rowsum(o_sel[r])) * o_sel[r]
             + (g_win / rowsum(o_win[r])) * o_win[r])
        heads.append(o[:, :HEAD_DIM])
    o_ref[0] = jnp.concatenate(heads, axis=1).astype(BF16)


def _attention(q, kc_aug, vc_p, ks_aug, vs_p, kw_aug, vw_p, gl, ovl):
    b, _, s, _ = q.shape
    nc = kc_aug.shape[2]
    tq = min(TQ, s)
    kv = lambda n: pl.BlockSpec((1, 1, n, LANES), lambda bi, gi, i: (bi, gi, 0, 0))
    return pl.pallas_call(
        _attn_kernel,
        grid=(b, N_KV, s // tq),
        in_specs=[pl.BlockSpec((1, GQA_R, tq, LANES), lambda bi, gi, i: (bi, gi, i, 0)),
                  kv(nc), kv(nc), kv(s), kv(s), kv(s), kv(s),
                  pl.BlockSpec((1, 1, tq, LANES), lambda bi, gi, i: (bi, gi, i, 0)),
                  pl.BlockSpec((N_BLK, nc), lambda bi, gi, i: (0, 0))],
        out_specs=pl.BlockSpec((1, tq, GQA_R * HEAD_DIM), lambda bi, gi, i: (bi, i, gi)),
        out_shape=jax.ShapeDtypeStruct((b, s, NSA_WIDTH), BF16),
        compiler_params=_cparams(3),
        name="nsa_attention",
    )(q, kc_aug, vc_p, ks_aug, vs_p, kw_aug, vw_p, gl, ovl)


def _merge_kernel(x_ref, mod_ref, gpre_ref, gpost_ref, on_ref, u_ref, uprev_ref, wbrg_ref,
                  wpool_ref, pscale_ref, wbn_ref, wbp_ref, wout_ref, o_ref):
    tm = x_ref.shape[1]
    i = pl.program_id(1)
    x = x_ref[0]
    h = (_rms(x) * gpre_ref[...]) * (1.0 + mod_ref[0, 1:2, :]) + mod_ref[0, 0:1, :]
    br = jax.nn.sigmoid(_dot(h.astype(BF16), wbrg_ref[...]))

    u = u_ref[0]
    prev = jnp.where(i > 0, uprev_ref[0], 0.0)
    ext = jnp.concatenate([prev, u], axis=0)
    tpos = i * tm + lax.broadcasted_iota(jnp.int32, (tm, 1), 0)
    mixed = []
    for gidx, w in enumerate(POOL_SIZES):
        acc = ext[:, gidx * POOL_GROUP:(gidx + 1) * POOL_GROUP]
        span = 1
        while span < w:
            acc = acc + pltpu.roll(acc, span, 0)
            span *= 2
        cnt = jnp.minimum(tpos + 1, w).astype(F32)
        pooled = acc[POOL_HALO:] / cnt - u[:, gidx * POOL_GROUP:(gidx + 1) * POOL_GROUP]
        mixed.append(_dot(pooled.astype(BF16), wpool_ref[gidx]))
    o_pool = jnp.concatenate(mixed, axis=1) * pscale_ref[...]

    merged = (br[:, :D_MODEL] * _dot(on_ref[0], wbn_ref[...])
              + br[:, D_MODEL:] * _dot(o_pool.astype(BF16), wbp_ref[...]))
    y = _dot(merged.astype(BF16), wout_ref[...])
    o_ref[0] = x + mod_ref[0, 2:3, :] * (_rms(y) * gpost_ref[...])


def _merge(x, mod, g_pre, g_post, o_nsa, u, w_brg, w_pool, pool_scale, w_br_nsa, w_br_pool, w_out):
    b, s, d = x.shape
    tm = min(TM_PROJ, s)
    halo_per_tile = tm // POOL_HALO
    tile = lambda w: pl.BlockSpec((1, tm, w), lambda bi, i: (bi, i, 0))
    c2 = lambda shape: pl.BlockSpec(shape, lambda bi, i: (0, 0))
    return pl.pallas_call(
        _merge_kernel,
        grid=(b, s // tm),
        in_specs=[tile(d),
                  pl.BlockSpec((1, N_MOD, d), lambda bi, i: (bi, 0, 0)),
                  c2((1, d)), c2((1, d)),
                  tile(NSA_WIDTH), tile(POOL_WIDTH),
                  pl.BlockSpec((1, POOL_HALO, POOL_WIDTH),
                               lambda bi, i: (bi, jnp.maximum(i * halo_per_tile - 1, 0), 0)),
                  c2((d, 2 * d)),
                  pl.BlockSpec((len(POOL_SIZES), POOL_GROUP, POOL_GROUP), lambda bi, i: (0, 0, 0)),
                  c2((1, POOL_WIDTH)),
                  c2((NSA_WIDTH, d)), c2((POOL_WIDTH, d)), c2((d, d))],
        out_specs=tile(d),
        out_shape=jax.ShapeDtypeStruct((b, s, d), F32),
        compiler_params=_cparams(2),
        name="merge",
    )(x, mod, g_pre, g_post, o_nsa, u, u, w_brg, w_pool, pool_scale, w_br_nsa, w_br_pool, w_out)


def _mlp_kernel(x_ref, mod_ref, gpre_ref, gpost_ref, w1_ref, w2_ref, o_ref):
    x = x_ref[0]
    h = ((_rms(x) * gpre_ref[...]) * (1.0 + mod_ref[0, 4:5, :]) + mod_ref[0, 3:4, :]).astype(BF16)
    y = jnp.zeros(x.shape, F32)
    for c0 in range(0, D_FF, FF_CHUNK):
        a = jnp.maximum(_dot(h, w1_ref[:, c0:c0 + FF_CHUNK]), 0.0)
        y = y + _dot((a * a).astype(BF16), w2_ref[c0:c0 + FF_CHUNK, :])
    o_ref[0] = x + mod_ref[0, 5:6, :] * (_rms(y) * gpost_ref[...])


def _mlp(x, mod, g_pre, g_post, w_fc1, w_fc2):
    b, s, d = x.shape
    tm = min(TM_PROJ, s)
    tile = pl.BlockSpec((1, tm, d), lambda bi, i: (bi, i, 0))
    resident = lambda shape: pl.BlockSpec(shape, lambda bi, i: (0, 0), pipeline_mode=pl.Buffered(1))
    return pl.pallas_call(
        _mlp_kernel,
        grid=(b, s // tm),
        in_specs=[tile,
                  pl.BlockSpec((1, N_MOD, d), lambda bi, i: (bi, 0, 0)),
                  pl.BlockSpec((1, d), lambda bi, i: (0, 0)),
                  pl.BlockSpec((1, d), lambda bi, i: (0, 0)),
                  resident((d, D_FF)), resident((D_FF, d))],
        out_specs=tile,
        out_shape=jax.ShapeDtypeStruct((b, s, d), F32),
        compiler_params=_cparams(2),
        name="mlp",
    )(x, mod, g_pre, g_post, w_fc1, w_fc2)


def _rope_tables(s):
    half = HEAD_DIM // 2
    inv = ROPE_THETA ** (-jnp.arange(half, dtype=F32) / half)
    ang = jnp.arange(s, dtype=F32)[:, None] * inv[None, :]
    cos = jnp.cos(ang)
    sin = jnp.sin(ang)
    reps = LANES // HEAD_DIM
    cos_t = jnp.tile(jnp.concatenate([cos, cos], axis=1), (1, reps))
    sin_t = jnp.tile(jnp.concatenate([-sin, sin], axis=1), (1, reps))
    return cos_t, sin_t


def _overlap_table(nc, n_sel):
    cmp_start = np.arange(nc) * CMP_STRIDE
    sel_start = np.arange(n_sel) * SEL_BLOCK
    ovl = ((cmp_start[None, :] < sel_start[:, None] + SEL_BLOCK)
           & (cmp_start[None, :] + CMP_BLOCK > sel_start[:, None])).astype(np.float32)
    out = np.zeros((N_BLK, nc), np.float32)
    out[:n_sel] = ovl
    return jnp.asarray(out, dtype=BF16)


def _pack_in_weights(w_in):
    q, kc, vc, ks, vs, kw, vw, gate, u, brg = jnp.split(w_in, IN_SPLITS, axis=-1)
    d = w_in.shape[0]
    gate = gate.reshape(d, N_KV, GQA_R * 3)
    gate = jnp.pad(gate, ((0, 0), (0, 0), (0, LANES - GQA_R * 3))).reshape(d, N_KV * LANES)
    cat = jnp.concatenate([q * (HEAD_DIM ** -0.5 * LOG2E), kc, ks, kw, vc, vs, vw, gate, u], axis=1)
    return cat.astype(BF16), brg.astype(BF16)


def kernel(x, c, w_ada, b_ada, g_pre_mix, g_post_mix, w_in, cmp_pe_k, cmp_w1_k, cmp_w2_k,
           cmp_pe_v, cmp_w1_v, cmp_w2_v, w_pool, pool_scale, w_br_nsa, w_br_pool, w_out,
           g_pre_mlp, g_post_mlp, w_fc1, w_fc2):
    b, s, d = x.shape
    depth = w_ada.shape[0]
    n_sel = s // SEL_BLOCK
    nc = s // CMP_STRIDE
    assert d == D_MODEL and s % TM_PROJ == 0 and s >= WINDOW + TQ
    assert SEL_TOPK <= n_sel <= N_BLK
    cos_t, sin_t = _rope_tables(s)
    ovl = _overlap_table(nc, n_sel)

    for l in range(depth):
        mod = _modulation(c, w_ada, b_ada, l).reshape(b, N_MOD, d)
        w_cat, w_brg = _pack_in_weights(w_in[l])
        q, kc, vc, ks_aug, kw_aug, vs_p, vw_p, gl, u = _inproj(
            x, mod, g_pre_mix[l].reshape(1, d), w_cat, cos_t, sin_t)
        pek, w1k, w2k = _compress_weights(cmp_pe_k[l], cmp_w1_k[l], cmp_w2_k[l])
        pev, w1v, w2v = _compress_weights(cmp_pe_v[l], cmp_w1_v[l], cmp_w2_v[l])
        kc_aug, vc_p = _compress(kc, vc, pek, pev, w1k, w1v, w2k, w2v)
        o_nsa = _attention(q, kc_aug, vc_p, ks_aug, vs_p, kw_aug, vw_p, gl, ovl)
        x = _merge(x, mod, g_pre_mix[l].reshape(1, d), g_post_mix[l].reshape(1, d), o_nsa, u,
                   w_brg, w_pool[l].astype(BF16), pool_scale[l].reshape(1, POOL_WIDTH),
                   w_br_nsa[l].astype(BF16), w_br_pool[l].astype(BF16), w_out[l].astype(BF16))
        x = _mlp(x, mod, g_pre_mlp[l].reshape(1, d), g_post_mlp[l].reshape(1, d),
                 w_fc1[l].astype(BF16), w_fc2[l].astype(BF16))
    return x
```

```python
import functools

import numpy as np
import jax
import jax.numpy as jnp
from jax import lax
from jax.experimental import pallas as pl
from jax.experimental.pallas import tpu as pltpu

F32 = jnp.float32
BF16 = jnp.bfloat16

D_MODEL = 1024
N_HEADS = 8
N_KV = 2
HEAD_DIM = 64
GQA_R = N_HEADS // N_KV
NSA_WIDTH = N_HEADS * HEAD_DIM
KV_WIDTH = N_KV * HEAD_DIM
CMP_BLOCK = 32
CMP_STRIDE = 16
SEL_BLOCK = 64
SEL_TOPK = 16
N_FORCED_LOCAL = 2
WINDOW = 512
FORCE_SCORE = 1e9
POOL_SIZES = (2, 4, 8, 16)
POOL_GROUP = 128
POOL_WIDTH = POOL_GROUP * len(POOL_SIZES)
D_FF = 4 * D_MODEL
ROPE_THETA = 10000.0
EPS = 1e-6
N_MOD = 6
IN_SIZES = (NSA_WIDTH, KV_WIDTH, KV_WIDTH, KV_WIDTH, KV_WIDTH, KV_WIDTH, KV_WIDTH,
            3 * N_HEADS, POOL_WIDTH, 2 * D_MODEL)
IN_SPLITS = [int(v) for v in np.cumsum(IN_SIZES)[:-1]]

LANES = 128
SUBLANES = 8
N_BLK = LANES - HEAD_DIM
ROWSUM_LANE = HEAD_DIM
LOG2E = 1.4426950408889634
POOL_HALO = 16
BLOCK_MASK_BIAS = -float(2 ** 30)
NOT_CAUSAL_SCORE = -1e30

C_Q = 0
C_KC = C_Q + NSA_WIDTH
C_KS = C_KC + KV_WIDTH
C_KW = C_KS + KV_WIDTH
C_VC = C_KW + KV_WIDTH
C_VS = C_VC + KV_WIDTH
C_VW = C_VS + KV_WIDTH
C_GATE = C_VW + KV_WIDTH
C_U = C_GATE + N_KV * LANES
IN_COLS = C_U + POOL_WIDTH

TM_PROJ = 512
TQ = 256
TK_SEL = 1024
TK_SEL_DIAG = 512
FF_CHUNK = 1024
VMEM_LIMIT = 56 * 1024 * 1024


def _cparams(n_axes):
    return pltpu.CompilerParams(dimension_semantics=("arbitrary",) * n_axes,
                                vmem_limit_bytes=VMEM_LIMIT)


def _dot(a, b):
    return jnp.dot(a, b, preferred_element_type=F32)


def _dot_nt(a, b):
    return lax.dot_general(a, b, (((1,), (1,)), ((), ())), preferred_element_type=F32)


def _rms(x):
    return x * lax.rsqrt(jnp.mean(x * x, axis=-1, keepdims=True) + EPS)


def _mod_kernel(c_ref, w_ref, b_ref, o_ref):
    c = c_ref[...]
    a = c * jax.nn.sigmoid(c)
    o_ref[...] = jnp.dot(a, w_ref[...], preferred_element_type=F32,
                         precision=lax.Precision.HIGHEST) + b_ref[...]


def _modulation(c, w_ada, b_ada, layer):
    b, d = c.shape
    n = w_ada.shape[2]
    tn = 1024
    return pl.pallas_call(
        _mod_kernel,
        grid=(n // tn,),
        in_specs=[pl.BlockSpec((b, d), lambda j: (0, 0)),
                  pl.BlockSpec((None, d, tn), lambda j: (layer, 0, j)),
                  pl.BlockSpec((None, 1, tn), lambda j: (layer, 0, j))],
        out_specs=pl.BlockSpec((b, tn), lambda j: (0, j)),
        out_shape=jax.ShapeDtypeStruct((b, n), F32),
        compiler_params=_cparams(1),
        name="modulation",
    )(c, w_ada, b_ada.reshape(-1, 1, n))


def _inproj_kernel(x_ref, mod_ref, g_ref, w_ref, cos_ref, sin_ref,
                   q_ref, kc_ref, vc_ref, ks_ref, kw_ref, vs_ref, vw_ref, gl_ref, u_ref):
    tm = x_ref.shape[1]
    h = (_rms(x_ref[0]) * g_ref[...]) * (1.0 + mod_ref[0, 1:2, :]) + mod_ref[0, 0:1, :]
    p = _dot(h.astype(BF16), w_ref[...])

    lane = lax.broadcasted_iota(jnp.int32, (tm, LANES), 1)
    lo = lane < HEAD_DIM
    first_half = (lane % HEAD_DIM) < (HEAD_DIM // 2)
    cos = cos_ref[...]
    sin = sin_ref[...]

    def rope(v):
        rot = jnp.where(first_half, pltpu.roll(v, LANES - HEAD_DIM // 2, 1),
                        pltpu.roll(v, HEAD_DIM // 2, 1))
        return v * cos + rot * sin

    def swap(v):
        return pltpu.roll(v, HEAD_DIM, 1)

    def slab(c0):
        return p[:, c0:c0 + LANES]

    for j in range(N_HEADS // 2):
        v = rope(slab(C_Q + j * LANES))
        q_ref[0, 2 * j] = jnp.where(lo, v, 0.0).astype(BF16)
        q_ref[0, 2 * j + 1] = jnp.where(lo, swap(v), 0.0).astype(BF16)

    kc_ref[0] = rope(slab(C_KC))
    vc_ref[0] = slab(C_VC)

    pos = pl.program_id(1) * tm + lax.broadcasted_iota(jnp.int32, (tm, LANES), 0)
    onehot = jnp.where(lane - HEAD_DIM == pos // SEL_BLOCK, 1.0, 0.0)
    v = rope(slab(C_KS))
    ks_ref[0, 0] = jnp.where(lo, v, onehot).astype(BF16)
    ks_ref[0, 1] = jnp.where(lo, swap(v), onehot).astype(BF16)
    v = rope(slab(C_KW))
    kw_ref[0, 0] = jnp.where(lo, v, 0.0).astype(BF16)
    kw_ref[0, 1] = jnp.where(lo, swap(v), 0.0).astype(BF16)

    ones_col = jnp.where(lane == ROWSUM_LANE, 1.0, 0.0)
    for c0, ref in ((C_VS, vs_ref), (C_VW, vw_ref)):
        v = slab(c0)
        ref[0, 0] = jnp.where(lo, v, ones_col).astype(BF16)
        ref[0, 1] = jnp.where(lo, swap(v), ones_col).astype(BF16)

    for g in range(N_KV):
        gl_ref[0, g] = slab(C_GATE + g * LANES)
    u_ref[0] = p[:, C_U:C_U + POOL_WIDTH]


def _inproj(x, mod, g_pre, w_cat, cos_t, sin_t):
    b, s, d = x.shape
    tm = min(TM_PROJ, s)
    nt = s // tm
    const = lambda bi, i: (0, 0)
    out_shape = (
        jax.ShapeDtypeStruct((b, N_HEADS, s, LANES), BF16),
        jax.ShapeDtypeStruct((b, s, KV_WIDTH), F32),
        jax.ShapeDtypeStruct((b, s, KV_WIDTH), F32),
        jax.ShapeDtypeStruct((b, N_KV, s, LANES), BF16),
        jax.ShapeDtypeStruct((b, N_KV, s, LANES), BF16),
        jax.ShapeDtypeStruct((b, N_KV, s, LANES), BF16),
        jax.ShapeDtypeStruct((b, N_KV, s, LANES), BF16),
        jax.ShapeDtypeStruct((b, N_KV, s, LANES), F32),
        jax.ShapeDtypeStruct((b, s, POOL_WIDTH), F32),
    )
    kv4 = lambda w: pl.BlockSpec((1, N_KV, tm, w), lambda bi, i: (bi, 0, i, 0))
    tok = lambda w: pl.BlockSpec((1, tm, w), lambda bi, i: (bi, i, 0))
    out_specs = (
        pl.BlockSpec((1, N_HEADS, tm, LANES), lambda bi, i: (bi, 0, i, 0)),
        tok(KV_WIDTH), tok(KV_WIDTH), kv4(LANES), kv4(LANES), kv4(LANES), kv4(LANES), kv4(LANES),
        tok(POOL_WIDTH),
    )
    return pl.pallas_call(
        _inproj_kernel,
        grid=(b, nt),
        in_specs=[pl.BlockSpec((1, tm, d), lambda bi, i: (bi, i, 0)),
                  pl.BlockSpec((1, N_MOD, d), lambda bi, i: (bi, 0, 0)),
                  pl.BlockSpec((1, d), const),
                  pl.BlockSpec((d, IN_COLS), const),
                  pl.BlockSpec((tm, LANES), lambda bi, i: (i, 0)),
                  pl.BlockSpec((tm, LANES), lambda bi, i: (i, 0))],
        out_specs=out_specs,
        out_shape=out_shape,
        compiler_params=_cparams(2),
        name="inproj",
    )(x, mod, g_pre, w_cat, cos_t, sin_t)


def _gelu_tanh(x):
    return 0.5 * x * (1.0 + jnp.tanh(np.sqrt(2.0 / np.pi) * (x + 0.044715 * (x * x * x))))


def _compress_kernel(k_ref, v_ref, pek_ref, pev_ref, w1k_ref, w1v_ref, w2k_ref, w2v_ref,
                     kc_ref, vc_ref):
    nc = kc_ref.shape[2]
    lane = lax.broadcasted_iota(jnp.int32, (nc, LANES), 1)
    lo = lane < HEAD_DIM

    def run(x_ref, pe_ref, w1_ref, w2_ref):
        chunks = jnp.concatenate(
            [x_ref[0, pl.ds(l, nc, stride=CMP_STRIDE), :] for l in range(CMP_STRIDE)], axis=1)
        pa = _dot((chunks + pe_ref[0:1, :]).astype(BF16), w1_ref[0])
        pb = _dot((chunks + pe_ref[1:2, :]).astype(BF16), w1_ref[1])
        pb = jnp.concatenate([pb[1:], jnp.zeros((1, LANES), F32)], axis=0)
        hid = _gelu_tanh(pa + pb)
        out = _dot(hid.astype(BF16), w2_ref[...])
        return (jnp.where(lo, out, 0.0).astype(BF16),
                jnp.where(lo, pltpu.roll(out, HEAD_DIM, 1), 0.0).astype(BF16))

    kc_ref[0, 0], kc_ref[0, 1] = run(k_ref, pek_ref, w1k_ref, w2k_ref)
    vc_ref[0, 0], vc_ref[0, 1] = run(v_ref, pev_ref, w1v_ref, w2v_ref)


def _compress_weights(pe, w1, w2):
    eye = jnp.eye(N_KV, dtype=F32)
    pe2 = jnp.tile(pe.reshape(2, CMP_STRIDE, 1, HEAD_DIM), (1, 1, N_KV, 1))
    pe2 = pe2.reshape(2, CMP_STRIDE * KV_WIDTH)
    w1r = w1.reshape(2, CMP_STRIDE, HEAD_DIM, HEAD_DIM)
    w1b = jnp.einsum("hlde,gk->hlgdke", w1r, eye).reshape(2, CMP_STRIDE * KV_WIDTH, KV_WIDTH)
    w2b = jnp.einsum("de,gk->gdke", w2, eye).reshape(KV_WIDTH, KV_WIDTH)
    return pe2, w1b.astype(BF16), w2b.astype(BF16)


def _compress(kc, vc, pek, pev, w1k, w1v, w2k, w2v):
    b, s, width = kc.shape
    nc = s // CMP_STRIDE
    wide = CMP_STRIDE * width
    blk = pl.BlockSpec((1, s, width), lambda bi: (bi, 0, 0))
    c2 = lambda bi: (0, 0)
    c3 = lambda bi: (0, 0, 0)
    out = pl.BlockSpec((1, N_KV, nc, LANES), lambda bi: (bi, 0, 0, 0))
    return pl.pallas_call(
        _compress_kernel,
        grid=(b,),
        in_specs=[blk, blk,
                  pl.BlockSpec((2, wide), c2), pl.BlockSpec((2, wide), c2),
                  pl.BlockSpec((2, wide, width), c3), pl.BlockSpec((2, wide, width), c3),
                  pl.BlockSpec((width, width), c2), pl.BlockSpec((width, width), c2)],
        out_specs=(out, out),
        out_shape=(jax.ShapeDtypeStruct((b, N_KV, nc, LANES), BF16),
                   jax.ShapeDtypeStruct((b, N_KV, nc, LANES), BF16)),
        compiler_params=_cparams(1),
        name="compress",
    )(kc, vc, pek, pev, w1k, w1v, w2k, w2v)


def _attn_kernel(q_ref, kc_ref, vc_ref, ks_ref, vs_ref, kw_ref, vw_ref, gl_ref, ovl_ref, o_ref,
                 val_ref, rank_ref):
    tq = q_ref.shape[2]
    nc = kc_ref.shape[2]
    s_len = ks_ref.shape[2]
    rows = GQA_R * tq
    q0 = pl.program_id(2) * tq

    q4 = q_ref[0]
    t_col = q0 + lax.broadcasted_iota(jnp.int32, (tq, 1), 0)
    t_rows = jnp.concatenate([t_col] * GQA_R, axis=0)

    def head(a, r):
        return a[r * tq:(r + 1) * tq]

    def pv_heads(pb, v):
        return [_dot(head(pb, r), v) for r in range(GQA_R)]

    def weights(s, m):
        return jnp.exp2((s - m).astype(BF16))

    q_plain = q4.reshape(rows, LANES)
    sc = _dot_nt(q_plain, kc_ref[0, 0])
    cmp_end = lax.broadcasted_iota(jnp.int32, (1, nc), 1) * CMP_STRIDE + (CMP_BLOCK - 1)
    sc = jnp.where(cmp_end <= t_rows, sc, -jnp.inf)
    m = jnp.max(sc, axis=-1, keepdims=True)
    m = jnp.where(m == -jnp.inf, 0.0, m)
    e = jnp.exp2(sc - m)
    p_cmp = e / jnp.maximum(jnp.sum(e, axis=-1, keepdims=True), 1e-30)
    o_cmp = pv_heads(p_cmp.astype(BF16), vc_ref[0, 0])

    p_sum = p_cmp[0:tq] + p_cmp[tq:2 * tq] + p_cmp[2 * tq:3 * tq] + p_cmp[3 * tq:]
    p_hi = p_sum.astype(BF16)
    rem = p_sum - p_hi.astype(F32)
    p_mid = rem.astype(BF16)
    p_lo = (rem - p_mid.astype(F32)).astype(BF16)
    ovl_t = ovl_ref[...]
    imp = _dot_nt(ovl_t, p_hi) + _dot_nt(ovl_t, p_mid) + _dot_nt(ovl_t, p_lo)
    blk = lax.broadcasted_iota(jnp.int32, (N_BLK, tq), 0)
    cur = (q0 + lax.broadcasted_iota(jnp.int32, (1, tq), 1)) // SEL_BLOCK
    causal = blk <= cur
    forced = (blk == 0) | (causal & (blk > cur - N_FORCED_LOCAL))
    val = jnp.where(forced, FORCE_SCORE, jnp.where(causal, imp, NOT_CAUSAL_SCORE))

    n_tiles = N_BLK // SUBLANES
    row8 = lax.broadcasted_iota(jnp.int32, (SUBLANES, tq), 0)
    val_ref[...] = val
    rank_ref[...] = jnp.zeros((N_BLK, tq), F32)
    last_causal = (q0 + tq - 1) // SEL_BLOCK
    for src in range(n_tiles):
        @pl.when(src * SUBLANES <= last_causal)
        def _(src=src):
            rows_b = [jnp.broadcast_to(val_ref[mp:mp + 1, :], (SUBLANES, tq))
                      for mp in range(src * SUBLANES, (src + 1) * SUBLANES)]
            for dst in range(n_tiles):
                sl = slice(dst * SUBLANES, (dst + 1) * SUBLANES)
                v = val_ref[sl, :]
                acc = jnp.zeros((SUBLANES, tq), F32)
                for i, b in enumerate(rows_b):
                    if src < dst:
                        inc = jnp.where(b >= v, 1.0, 0.0)
                    elif src > dst:
                        inc = jnp.where(b > v, 1.0, 0.0)
                    else:
                        inc = jnp.where(row8 > i, jnp.where(b >= v, 1.0, 0.0),
                                        jnp.where(b > v, 1.0, 0.0))
                    acc = acc + inc
                rank_ref[sl, :] += acc
    rank = rank_ref[...]
    bias_t = jnp.where(rank < float(SEL_TOPK), 0.0, BLOCK_MASK_BIAS)
    bias = jnp.concatenate([jnp.zeros((HEAD_DIM, tq), F32), bias_t], axis=0).T
    q_aug = (q4 + bias.astype(BF16)[None]).reshape(rows, LANES)

    tk_wide = min(TK_SEL, s_len)

    def sel_step(j, carry, base, tk, diagonal):
        m_i, accs = carry
        k0 = pl.multiple_of(base + j * tk, tk)
        s = _dot_nt(q_aug, ks_ref[0, 0, pl.ds(k0, tk), :])
        if diagonal:
            kpos = k0 + lax.broadcasted_iota(jnp.int32, (1, tk), 1)
            s = jnp.where(kpos <= t_rows, s, -jnp.inf)
        m_new = jnp.maximum(m_i, jnp.max(s, axis=-1, keepdims=True))
        alpha = jnp.exp2(m_i - m_new)
        outs = pv_heads(weights(s, m_new), vs_ref[0, 0, pl.ds(k0, tk), :])
        return m_new, tuple(head(alpha, r) * accs[r] + outs[r] for r in range(GQA_R))

    tk_diag = min(TK_SEL_DIAG, s_len)
    n_wide = q0 // tk_wide
    diag0 = (q0 // tk_diag) * tk_diag
    n_mid = (diag0 - n_wide * tk_wide) // tk_diag
    zero_acc = jnp.zeros((tq, LANES), F32)
    carry = (jnp.full((rows, 1), -jnp.inf, F32), (zero_acc,) * GQA_R)
    carry = lax.fori_loop(
        0, n_wide, functools.partial(sel_step, base=0, tk=tk_wide, diagonal=False), carry)
    carry = lax.fori_loop(
        0, n_mid,
        functools.partial(sel_step, base=n_wide * tk_wide, tk=tk_diag, diagonal=False), carry)
    _, o_sel = sel_step(0, carry, base=diag0, tk=tk_diag, diagonal=True)

    wlen = min(WINDOW + tq, s_len)
    w0 = pl.multiple_of(jnp.maximum(q0 + tq - wlen, 0), tq)
    sw = _dot_nt(q_plain, kw_ref[0, 0, pl.ds(w0, wlen), :])
    kp = w0 + lax.broadcasted_iota(jnp.int32, (1, wlen), 1)
    age = lax.bitcast_convert_type(t_rows - kp[:, :tq], jnp.uint32)
    sw = jnp.concatenate([jnp.where(age < WINDOW, sw[:, :tq], -jnp.inf),
                          jnp.where(kp[:, tq:] <= t_rows, sw[:, tq:], -jnp.inf)], axis=1)
    o_win = pv_heads(weights(sw, jnp.max(sw, axis=-1, keepdims=True)),
                     vw_ref[0, 0, pl.ds(w0, wlen), :])

    sg = jax.nn.sigmoid(gl_ref[0, 0])
    rowsum = lambda a: a[:, ROWSUM_LANE:ROWSUM_LANE + 1]
    heads = []
    for r in range(GQA_R):
        g_cmp, g_sel, g_win = (sg[:, 3 * r + j:3 * r + j + 1] for j in range(3))
        o = (g_cmp * o_cmp[r] + (g_sel / rowsum(o_sel[r])) * o_sel[r]
             + (g_win / rowsum(o_win[r])) * o_win[r])
        heads.append(o[:, :HEAD_DIM])
    o_ref[0] = jnp.concatenate(heads, axis=1).astype(BF16)


def _attention(q, kc_aug, vc_p, ks_aug, vs_p, kw_aug, vw_p, gl, ovl):
    b, _, s, _ = q.shape
    nc = kc_aug.shape[2]
    tq = min(TQ, s)
    kv = lambda n: pl.BlockSpec((1, 1, n, LANES), lambda bi, gi, i: (bi, gi, 0, 0))
    return pl.pallas_call(
        _attn_kernel,
        grid=(b, N_KV, s // tq),
        in_specs=[pl.BlockSpec((1, GQA_R, tq, LANES), lambda bi, gi, i: (bi, gi, i, 0)),
                  kv(nc), kv(nc), kv(s), kv(s), kv(s), kv(s),
                  pl.BlockSpec((1, 1, tq, LANES), lambda bi, gi, i: (bi, gi, i, 0)),
                  pl.BlockSpec((N_BLK, nc), lambda bi, gi, i: (0, 0))],
        out_specs=pl.BlockSpec((1, tq, GQA_R * HEAD_DIM), lambda bi, gi, i: (bi, i, gi)),
        out_shape=jax.ShapeDtypeStruct((b, s, NSA_WIDTH), BF16),
        scratch_shapes=[pltpu.VMEM((N_BLK, tq), F32), pltpu.VMEM((N_BLK, tq), F32)],
        compiler_params=_cparams(3),
        name="nsa_attention",
    )(q, kc_aug, vc_p, ks_aug, vs_p, kw_aug, vw_p, gl, ovl)


def _merge_kernel(x_ref, mod_ref, gpre_ref, gpost_ref, on_ref, u_ref, uprev_ref, wbrg_ref,
                  wpool_ref, pscale_ref, wbn_ref, wbp_ref, wout_ref, o_ref):
    tm = x_ref.shape[1]
    i = pl.program_id(1)
    x = x_ref[0]
    h = (_rms(x) * gpre_ref[...]) * (1.0 + mod_ref[0, 1:2, :]) + mod_ref[0, 0:1, :]
    br = jax.nn.sigmoid(_dot(h.astype(BF16), wbrg_ref[...]))

    u = u_ref[0]
    prev = jnp.where(i > 0, uprev_ref[0], 0.0)
    ext = jnp.concatenate([prev, u], axis=0)
    tpos = i * tm + lax.broadcasted_iota(jnp.int32, (tm, 1), 0)
    mixed = []
    for gidx, w in enumerate(POOL_SIZES):
        acc = ext[:, gidx * POOL_GROUP:(gidx + 1) * POOL_GROUP]
        span = 1
        while span < w:
            acc = acc + pltpu.roll(acc, span, 0)
            span *= 2
        cnt = jnp.minimum(tpos + 1, w).astype(F32)
        pooled = acc[POOL_HALO:] / cnt - u[:, gidx * POOL_GROUP:(gidx + 1) * POOL_GROUP]
        mixed.append(_dot(pooled.astype(BF16), wpool_ref[gidx]))
    o_pool = jnp.concatenate(mixed, axis=1) * pscale_ref[...]

    merged = (br[:, :D_MODEL] * _dot(on_ref[0], wbn_ref[...])
              + br[:, D_MODEL:] * _dot(o_pool.astype(BF16), wbp_ref[...]))
    y = _dot(merged.astype(BF16), wout_ref[...])
    o_ref[0] = x + mod_ref[0, 2:3, :] * (_rms(y) * gpost_ref[...])


def _merge(x, mod, g_pre, g_post, o_nsa, u, w_brg, w_pool, pool_scale, w_br_nsa, w_br_pool, w_out):
    b, s, d = x.shape
    tm = min(TM_PROJ, s)
    halo_per_tile = tm // POOL_HALO
    tile = lambda w: pl.BlockSpec((1, tm, w), lambda bi, i: (bi, i, 0))
    c2 = lambda shape: pl.BlockSpec(shape, lambda bi, i: (0, 0))
    return pl.pallas_call(
        _merge_kernel,
        grid=(b, s // tm),
        in_specs=[tile(d),
                  pl.BlockSpec((1, N_MOD, d), lambda bi, i: (bi, 0, 0)),
                  c2((1, d)), c2((1, d)),
                  tile(NSA_WIDTH), tile(POOL_WIDTH),
                  pl.BlockSpec((1, POOL_HALO, POOL_WIDTH),
                               lambda bi, i: (bi, jnp.maximum(i * halo_per_tile - 1, 0), 0)),
                  c2((d, 2 * d)),
                  pl.BlockSpec((len(POOL_SIZES), POOL_GROUP, POOL_GROUP), lambda bi, i: (0, 0, 0)),
                  c2((1, POOL_WIDTH)),
                  c2((NSA_WIDTH, d)), c2((POOL_WIDTH, d)), c2((d, d))],
        out_specs=tile(d),
        out_shape=jax.ShapeDtypeStruct((b, s, d), F32),
        compiler_params=_cparams(2),
        name="merge",
    )(x, mod, g_pre, g_post, o_nsa, u, u, w_brg, w_pool, pool_scale, w_br_nsa, w_br_pool, w_out)


def _mlp_kernel(x_ref, mod_ref, gpre_ref, gpost_ref, w1_ref, w2_ref, o_ref):
    x = x_ref[0]
    h = ((_rms(x) * gpre_ref[...]) * (1.0 + mod_ref[0, 4:5, :]) + mod_ref[0, 3:4, :]).astype(BF16)
    y = jnp.zeros(x.shape, F32)
    for c0 in range(0, D_FF, FF_CHUNK):
        a = jnp.maximum(_dot(h, w1_ref[:, c0:c0 + FF_CHUNK]), 0.0)
        y = y + _dot((a * a).astype(BF16), w2_ref[c0:c0 + FF_CHUNK, :])
    o_ref[0] = x + mod_ref[0, 5:6, :] * (_rms(y) * gpost_ref[...])


def _mlp(x, mod, g_pre, g_post, w_fc1, w_fc2):
    b, s, d = x.shape
    tm = min(TM_PROJ, s)
    tile = pl.BlockSpec((1, tm, d), lambda bi, i: (bi, i, 0))
    resident = lambda shape: pl.BlockSpec(shape, lambda bi, i: (0, 0), pipeline_mode=pl.Buffered(1))
    return pl.pallas_call(
        _mlp_kernel,
        grid=(b, s // tm),
        in_specs=[tile,
                  pl.BlockSpec((1, N_MOD, d), lambda bi, i: (bi, 0, 0)),
                  pl.BlockSpec((1, d), lambda bi, i: (0, 0)),
                  pl.BlockSpec((1, d), lambda bi, i: (0, 0)),
                  resident((d, D_FF)), resident((D_FF, d))],
        out_specs=tile,
        out_shape=jax.ShapeDtypeStruct((b, s, d), F32),
        compiler_params=_cparams(2),
        name="mlp",
    )(x, mod, g_pre, g_post, w_fc1, w_fc2)


def _rope_tables(s):
    half = HEAD_DIM // 2
    inv = ROPE_THETA ** (-jnp.arange(half, dtype=F32) / half)
    ang = jnp.arange(s, dtype=F32)[:, None] * inv[None, :]
    cos = jnp.cos(ang)
    sin = jnp.sin(ang)
    reps = LANES // HEAD_DIM
    cos_t = jnp.tile(jnp.concatenate([cos, cos], axis=1), (1, reps))
    sin_t = jnp.tile(jnp.concatenate([-sin, sin], axis=1), (1, reps))
    return cos_t, sin_t


def _overlap_table(nc, n_sel):
    cmp_start = np.arange(nc) * CMP_STRIDE
    sel_start = np.arange(n_sel) * SEL_BLOCK
    ovl = ((cmp_start[None, :] < sel_start[:, None] + SEL_BLOCK)
           & (cmp_start[None, :] + CMP_BLOCK > sel_start[:, None])).astype(np.float32)
    out = np.zeros((N_BLK, nc), np.float32)
    out[:n_sel] = ovl
    return jnp.asarray(out, dtype=BF16)


def _pack_in_weights(w_in):
    q, kc, vc, ks, vs, kw, vw, gate, u, brg = jnp.split(w_in, IN_SPLITS, axis=-1)
    d = w_in.shape[0]
    gate = gate.reshape(d, N_KV, GQA_R * 3)
    gate = jnp.pad(gate, ((0, 0), (0, 0), (0, LANES - GQA_R * 3))).reshape(d, N_KV * LANES)
    cat = jnp.concatenate([q * (HEAD_DIM ** -0.5 * LOG2E), kc, ks, kw, vc, vs, vw, gate, u], axis=1)
    return cat.astype(BF16), brg.astype(BF16)


def kernel(x, c, w_ada, b_ada, g_pre_mix, g_post_mix, w_in, cmp_pe_k, cmp_w1_k, cmp_w2_k,
           cmp_pe_v, cmp_w1_v, cmp_w2_v, w_pool, pool_scale, w_br_nsa, w_br_pool, w_out,
           g_pre_mlp, g_post_mlp, w_fc1, w_fc2):
    b, s, d = x.shape
    depth = w_ada.shape[0]
    n_sel = s // SEL_BLOCK
    nc = s // CMP_STRIDE
    assert d == D_MODEL and s % TM_PROJ == 0 and s >= WINDOW + TQ
    assert SEL_TOPK <= n_sel <= N_BLK
    cos_t, sin_t = _rope_tables(s)
    ovl = _overlap_table(nc, n_sel)

    for l in range(depth):
        mod = _modulation(c, w_ada, b_ada, l).reshape(b, N_MOD, d)
        w_cat, w_brg = _pack_in_weights(w_in[l])
        q, kc, vc, ks_aug, kw_aug, vs_p, vw_p, gl, u = _inproj(
            x, mod, g_pre_mix[l].reshape(1, d), w_cat, cos_t, sin_t)
        pek, w1k, w2k = _compress_weights(cmp_pe_k[l], cmp_w1_k[l], cmp_w2_k[l])
        pev, w1v, w2v = _compress_weights(cmp_pe_v[l], cmp_w1_v[l], cmp_w2_v[l])
        kc_aug, vc_p = _compress(kc, vc, pek, pev, w1k, w1v, w2k, w2v)
        o_nsa = _attention(q, kc_aug, vc_p, ks_aug, vs_p, kw_aug, vw_p, gl, ovl)
        x = _merge(x, mod, g_pre_mix[l].reshape(1, d), g_post_mix[l].reshape(1, d), o_nsa, u,
                   w_brg, w_pool[l].astype(BF16), pool_scale[l].reshape(1, POOL_WIDTH),
                   w_br_nsa[l].astype(BF16), w_br_pool[l].astype(BF16), w_out[l].astype(BF16))
        x = _mlp(x, mod, g_pre_mlp[l].reshape(1, d), g_post_mlp[l].reshape(1, d),
                 w_fc1[l].astype(BF16), w_fc2[l].astype(BF16))
    return x
```

```python
import functools

import numpy as np
import jax
import jax.numpy as jnp
from jax import lax
from jax.experimental import pallas as pl
from jax.experimental.pallas import tpu as pltpu

F32 = jnp.float32
BF16 = jnp.bfloat16

D_MODEL = 1024
N_HEADS = 8
N_KV = 2
HEAD_DIM = 64
GQA_R = N_HEADS // N_KV
NSA_WIDTH = N_HEADS * HEAD_DIM
KV_WIDTH = N_KV * HEAD_DIM
CMP_BLOCK = 32
CMP_STRIDE = 16
SEL_BLOCK = 64
SEL_TOPK = 16
N_FORCED_LOCAL = 2
WINDOW = 512
FORCE_SCORE = 1e9
POOL_SIZES = (2, 4, 8, 16)
POOL_GROUP = 128
POOL_WIDTH = POOL_GROUP * len(POOL_SIZES)
D_FF = 4 * D_MODEL
ROPE_THETA = 10000.0
EPS = 1e-6
N_MOD = 6
IN_SIZES = (NSA_WIDTH, KV_WIDTH, KV_WIDTH, KV_WIDTH, KV_WIDTH, KV_WIDTH, KV_WIDTH,
            3 * N_HEADS, POOL_WIDTH, 2 * D_MODEL)
IN_SPLITS = [int(v) for v in np.cumsum(IN_SIZES)[:-1]]

LANES = 128
SUBLANES = 8
N_BLK = LANES - HEAD_DIM
ROWSUM_LANE = HEAD_DIM
LOG2E = 1.4426950408889634
POOL_HALO = 16
BLOCK_MASK_BIAS = -float(2 ** 30)
NOT_CAUSAL_SCORE = -1e30

C_Q = 0
C_KC = C_Q + NSA_WIDTH
C_KS = C_KC + KV_WIDTH
C_KW = C_KS + KV_WIDTH
C_VC = C_KW + KV_WIDTH
C_VS = C_VC + KV_WIDTH
C_VW = C_VS + KV_WIDTH
C_GATE = C_VW + KV_WIDTH
C_U = C_GATE + LANES
IN_COLS = C_U + POOL_WIDTH

TM_PROJ = 512
TQ = 256
TK_SEL = 1024
TK_SEL_DIAG = 512
FF_CHUNK = 1024
VMEM_LIMIT = 56 * 1024 * 1024


def _cparams(n_axes):
    return pltpu.CompilerParams(dimension_semantics=("arbitrary",) * n_axes,
                                vmem_limit_bytes=VMEM_LIMIT)


def _dot(a, b):
    return jnp.dot(a, b, preferred_element_type=F32)


def _dot_nt(a, b):
    return lax.dot_general(a, b, (((1,), (1,)), ((), ())), preferred_element_type=F32)


def _rms(x):
    return x * lax.rsqrt(jnp.mean(x * x, axis=-1, keepdims=True) + EPS)


def _mod_kernel(c_ref, w_ref, b_ref, o_ref):
    c = c_ref[...]
    a = c * jax.nn.sigmoid(c)
    o_ref[...] = jnp.dot(a, w_ref[...], preferred_element_type=F32,
                         precision=lax.Precision.HIGHEST) + b_ref[...]


def _modulation(c, w_ada, b_ada, layer):
    b, d = c.shape
    n = w_ada.shape[2]
    tn = 2048
    return pl.pallas_call(
        _mod_kernel,
        grid=(n // tn,),
        in_specs=[pl.BlockSpec((b, d), lambda j: (0, 0)),
                  pl.BlockSpec((None, d, tn), lambda j: (layer, 0, j)),
                  pl.BlockSpec((None, 1, tn), lambda j: (layer, 0, j))],
        out_specs=pl.BlockSpec((b, tn), lambda j: (0, j)),
        out_shape=jax.ShapeDtypeStruct((b, n), F32),
        compiler_params=_cparams(1),
        name="modulation",
    )(c, w_ada, b_ada.reshape(-1, 1, n))


def _inproj_kernel(x_ref, mod_ref, g_ref, w_ref, cos_ref, sin_ref,
                   q_ref, kc_ref, vc_ref, ks_ref, kw_ref, vs_ref, vw_ref, gl_ref, u_ref):
    tm = x_ref.shape[1]
    h = (_rms(x_ref[0]) * g_ref[...]) * (1.0 + mod_ref[0, 1:2, :]) + mod_ref[0, 0:1, :]
    p = _dot(h.astype(BF16), w_ref[...])

    lane = lax.broadcasted_iota(jnp.int32, (tm, LANES), 1)
    lo = lane < HEAD_DIM
    first_half = (lane % HEAD_DIM) < (HEAD_DIM // 2)
    cos = cos_ref[...]
    sin = sin_ref[...]

    def rope(v):
        rot = jnp.where(first_half, pltpu.roll(v, LANES - HEAD_DIM // 2, 1),
                        pltpu.roll(v, HEAD_DIM // 2, 1))
        return v * cos + rot * sin

    def swap(v):
        return pltpu.roll(v, HEAD_DIM, 1)

    def slab(c0):
        return p[:, c0:c0 + LANES]

    for j in range(N_HEADS // 2):
        v = rope(slab(C_Q + j * LANES))
        q_ref[0, 2 * j] = jnp.where(lo, v, 0.0).astype(BF16)
        q_ref[0, 2 * j + 1] = jnp.where(lo, swap(v), 0.0).astype(BF16)

    kc_ref[0] = rope(slab(C_KC))
    vc_ref[0] = slab(C_VC)

    pos = pl.program_id(1) * tm + lax.broadcasted_iota(jnp.int32, (tm, LANES), 0)
    onehot = jnp.where(lane - HEAD_DIM == pos // SEL_BLOCK, 1.0, 0.0)
    v = rope(slab(C_KS))
    ks_ref[0, 0] = jnp.where(lo, v, onehot).astype(BF16)
    ks_ref[0, 1] = jnp.where(lo, swap(v), onehot).astype(BF16)
    v = rope(slab(C_KW))
    kw_ref[0, 0] = jnp.where(lo, v, 0.0).astype(BF16)
    kw_ref[0, 1] = jnp.where(lo, swap(v), 0.0).astype(BF16)

    ones_col = jnp.where(lane == ROWSUM_LANE, 1.0, 0.0)
    for c0, ref in ((C_VS, vs_ref), (C_VW, vw_ref)):
        v = slab(c0)
        ref[0, 0] = jnp.where(lo, v, ones_col).astype(BF16)
        ref[0, 1] = jnp.where(lo, swap(v), ones_col).astype(BF16)

    gates = slab(C_GATE)
    gl_ref[0, 0] = gates
    gl_ref[0, 1] = swap(gates)
    u_ref[0] = p[:, C_U:C_U + POOL_WIDTH]


def _inproj(x, mod, g_pre, w_cat, cos_t, sin_t):
    b, s, d = x.shape
    tm = min(TM_PROJ, s)
    nt = s // tm
    const = lambda bi, i: (0, 0)
    out_shape = (
        jax.ShapeDtypeStruct((b, N_HEADS, s, LANES), BF16),
        jax.ShapeDtypeStruct((b, s, KV_WIDTH), F32),
        jax.ShapeDtypeStruct((b, s, KV_WIDTH), F32),
        jax.ShapeDtypeStruct((b, N_KV, s, LANES), BF16),
        jax.ShapeDtypeStruct((b, N_KV, s, LANES), BF16),
        jax.ShapeDtypeStruct((b, N_KV, s, LANES), BF16),
        jax.ShapeDtypeStruct((b, N_KV, s, LANES), BF16),
        jax.ShapeDtypeStruct((b, N_KV, s, LANES), F32),
        jax.ShapeDtypeStruct((b, s, POOL_WIDTH), F32),
    )
    kv4 = lambda w: pl.BlockSpec((1, N_KV, tm, w), lambda bi, i: (bi, 0, i, 0))
    tok = lambda w: pl.BlockSpec((1, tm, w), lambda bi, i: (bi, i, 0))
    out_specs = (
        pl.BlockSpec((1, N_HEADS, tm, LANES), lambda bi, i: (bi, 0, i, 0)),
        tok(KV_WIDTH), tok(KV_WIDTH), kv4(LANES), kv4(LANES), kv4(LANES), kv4(LANES), kv4(LANES),
        tok(POOL_WIDTH),
    )
    return pl.pallas_call(
        _inproj_kernel,
        grid=(b, nt),
        in_specs=[pl.BlockSpec((1, tm, d), lambda bi, i: (bi, i, 0)),
                  pl.BlockSpec((1, N_MOD, d), lambda bi, i: (bi, 0, 0)),
                  pl.BlockSpec((1, d), const),
                  pl.BlockSpec((d, IN_COLS), const),
                  pl.BlockSpec((tm, LANES), lambda bi, i: (i, 0)),
                  pl.BlockSpec((tm, LANES), lambda bi, i: (i, 0))],
        out_specs=out_specs,
        out_shape=out_shape,
        compiler_params=_cparams(2),
        name="inproj",
    )(x, mod, g_pre, w_cat, cos_t, sin_t)


def _gelu_tanh(x):
    return 0.5 * x * (1.0 + jnp.tanh(np.sqrt(2.0 / np.pi) * (x + 0.044715 * (x * x * x))))


def _compress_kernel(k_ref, v_ref, pek_ref, pev_ref, w1k_ref, w1v_ref, w2k_ref, w2v_ref,
                     kc_ref, vc_ref):
    nc = kc_ref.shape[2]
    lane = lax.broadcasted_iota(jnp.int32, (nc, LANES), 1)
    lo = lane < HEAD_DIM

    def run(x_ref, pe_ref, w1_ref, w2_ref):
        chunks = jnp.concatenate(
            [x_ref[0, pl.ds(l, nc, stride=CMP_STRIDE), :] for l in range(CMP_STRIDE)], axis=1)
        pa = _dot((chunks + pe_ref[0:1, :]).astype(BF16), w1_ref[0])
        pb = _dot((chunks + pe_ref[1:2, :]).astype(BF16), w1_ref[1])
        pb = jnp.concatenate([pb[1:], jnp.zeros((1, LANES), F32)], axis=0)
        hid = _gelu_tanh(pa + pb)
        out = _dot(hid.astype(BF16), w2_ref[...])
        return (jnp.where(lo, out, 0.0).astype(BF16),
                jnp.where(lo, pltpu.roll(out, HEAD_DIM, 1), 0.0).astype(BF16))

    kc_ref[0, 0], kc_ref[0, 1] = run(k_ref, pek_ref, w1k_ref, w2k_ref)
    vc_ref[0, 0], vc_ref[0, 1] = run(v_ref, pev_ref, w1v_ref, w2v_ref)


def _compress_weights(pe, w1, w2):
    eye = jnp.eye(N_KV, dtype=F32)
    pe2 = jnp.tile(pe.reshape(2, CMP_STRIDE, 1, HEAD_DIM), (1, 1, N_KV, 1))
    pe2 = pe2.reshape(2, CMP_STRIDE * KV_WIDTH)
    w1r = w1.reshape(2, CMP_STRIDE, HEAD_DIM, HEAD_DIM)
    w1b = jnp.einsum("hlde,gk->hlgdke", w1r, eye).reshape(2, CMP_STRIDE * KV_WIDTH, KV_WIDTH)
    w2b = jnp.einsum("de,gk->gdke", w2, eye).reshape(KV_WIDTH, KV_WIDTH)
    return pe2, w1b.astype(BF16), w2b.astype(BF16)


def _compress(kc, vc, pek, pev, w1k, w1v, w2k, w2v):
    b, s, width = kc.shape
    nc = s // CMP_STRIDE
    wide = CMP_STRIDE * width
    blk = pl.BlockSpec((1, s, width), lambda bi: (bi, 0, 0))
    c2 = lambda bi: (0, 0)
    c3 = lambda bi: (0, 0, 0)
    out = pl.BlockSpec((1, N_KV, nc, LANES), lambda bi: (bi, 0, 0, 0))
    return pl.pallas_call(
        _compress_kernel,
        grid=(b,),
        in_specs=[blk, blk,
                  pl.BlockSpec((2, wide), c2), pl.BlockSpec((2, wide), c2),
                  pl.BlockSpec((2, wide, width), c3), pl.BlockSpec((2, wide, width), c3),
                  pl.BlockSpec((width, width), c2), pl.BlockSpec((width, width), c2)],
        out_specs=(out, out),
        out_shape=(jax.ShapeDtypeStruct((b, N_KV, nc, LANES), BF16),
                   jax.ShapeDtypeStruct((b, N_KV, nc, LANES), BF16)),
        compiler_params=_cparams(1),
        name="compress",
    )(kc, vc, pek, pev, w1k, w1v, w2k, w2v)


def _attn_kernel(q_ref, kc_ref, vc_ref, ks_ref, vs_ref, kw_ref, vw_ref, gl_ref, ovl_ref, o_ref,
                 val_ref, rank_ref):
    tq = q_ref.shape[2]
    nc = kc_ref.shape[2]
    s_len = ks_ref.shape[2]
    rows = GQA_R * tq
    q0 = pl.program_id(2) * tq

    q4 = q_ref[0]
    t_col = q0 + lax.broadcasted_iota(jnp.int32, (tq, 1), 0)
    t_rows = jnp.concatenate([t_col] * GQA_R, axis=0)

    def head(a, r):
        return a[r * tq:(r + 1) * tq]

    def pv_heads(pb, v):
        return [_dot(head(pb, r), v) for r in range(GQA_R)]

    def weights(s, m):
        return jnp.exp2((s - m).astype(BF16))

    q_plain = q4.reshape(rows, LANES)
    sc = _dot_nt(q_plain, kc_ref[0, 0])
    cmp_end = lax.broadcasted_iota(jnp.int32, (1, nc), 1) * CMP_STRIDE + (CMP_BLOCK - 1)
    sc = jnp.where(cmp_end <= t_rows, sc, -jnp.inf)
    m = jnp.max(sc, axis=-1, keepdims=True)
    m = jnp.where(m == -jnp.inf, 0.0, m)
    e = jnp.exp2(sc - m)
    p_cmp = e / jnp.maximum(jnp.sum(e, axis=-1, keepdims=True), 1e-30)
    o_cmp = pv_heads(p_cmp.astype(BF16), vc_ref[0, 0])

    p_sum = p_cmp[0:tq] + p_cmp[tq:2 * tq] + p_cmp[2 * tq:3 * tq] + p_cmp[3 * tq:]
    p_hi = p_sum.astype(BF16)
    rem = p_sum - p_hi.astype(F32)
    p_mid = rem.astype(BF16)
    p_lo = (rem - p_mid.astype(F32)).astype(BF16)
    ovl_t = ovl_ref[...]
    imp = _dot_nt(ovl_t, p_hi) + _dot_nt(ovl_t, p_mid) + _dot_nt(ovl_t, p_lo)
    blk = lax.broadcasted_iota(jnp.int32, (N_BLK, tq), 0)
    cur = (q0 + lax.broadcasted_iota(jnp.int32, (1, tq), 1)) // SEL_BLOCK
    causal = blk <= cur
    forced = (blk == 0) | (causal & (blk > cur - N_FORCED_LOCAL))
    val = jnp.where(forced, FORCE_SCORE, jnp.where(causal, imp, NOT_CAUSAL_SCORE))

    n_tiles = N_BLK // SUBLANES
    row8 = lax.broadcasted_iota(jnp.int32, (SUBLANES, tq), 0)
    val_ref[...] = val
    rank_ref[...] = jnp.zeros((N_BLK, tq), F32)
    last_causal = (q0 + tq - 1) // SEL_BLOCK
    for src in range(n_tiles):
        @pl.when(src * SUBLANES <= last_causal)
        def _(src=src):
            rows_b = [jnp.broadcast_to(val_ref[mp:mp + 1, :], (SUBLANES, tq))
                      for mp in range(src * SUBLANES, (src + 1) * SUBLANES)]
            for dst in range(n_tiles):
                sl = slice(dst * SUBLANES, (dst + 1) * SUBLANES)
                v = val_ref[sl, :]
                acc = jnp.zeros((SUBLANES, tq), F32)
                for i, b in enumerate(rows_b):
                    if src < dst:
                        inc = jnp.where(b >= v, 1.0, 0.0)
                    elif src > dst:
                        inc = jnp.where(b > v, 1.0, 0.0)
                    else:
                        inc = jnp.where(row8 > i, jnp.where(b >= v, 1.0, 0.0),
                                        jnp.where(b > v, 1.0, 0.0))
                    acc = acc + inc
                rank_ref[sl, :] += acc
    rank = rank_ref[...]
    bias_t = jnp.where(rank < float(SEL_TOPK), 0.0, BLOCK_MASK_BIAS)
    bias = jnp.concatenate([jnp.zeros((HEAD_DIM, tq), F32), bias_t], axis=0).T
    q_aug = (q4 + bias.astype(BF16)[None]).reshape(rows, LANES)

    tk_wide = min(TK_SEL, s_len)

    def sel_step(j, carry, base, tk, diagonal):
        m_i, accs = carry
        k0 = pl.multiple_of(base + j * tk, tk)
        s = _dot_nt(q_aug, ks_ref[0, 0, pl.ds(k0, tk), :])
        if diagonal:
            kpos = k0 + lax.broadcasted_iota(jnp.int32, (1, tk), 1)
            s = jnp.where(kpos <= t_rows, s, -jnp.inf)
        m_new = jnp.maximum(m_i, jnp.max(s, axis=-1, keepdims=True))
        alpha = jnp.exp2(m_i - m_new)
        outs = pv_heads(weights(s, m_new), vs_ref[0, 0, pl.ds(k0, tk), :])
        return m_new, tuple(head(alpha, r) * accs[r] + outs[r] for r in range(GQA_R))

    tk_diag = min(TK_SEL_DIAG, s_len)
    n_wide = q0 // tk_wide
    diag0 = (q0 // tk_diag) * tk_diag
    n_mid = (diag0 - n_wide * tk_wide) // tk_diag
    zero_acc = jnp.zeros((tq, LANES), F32)
    carry = (jnp.full((rows, 1), -jnp.inf, F32), (zero_acc,) * GQA_R)
    carry = lax.fori_loop(
        0, n_wide, functools.partial(sel_step, base=0, tk=tk_wide, diagonal=False), carry)
    carry = lax.fori_loop(
        0, n_mid,
        functools.partial(sel_step, base=n_wide * tk_wide, tk=tk_diag, diagonal=False), carry)
    _, o_sel = sel_step(0, carry, base=diag0, tk=tk_diag, diagonal=True)

    wlen = min(WINDOW + tq, s_len)
    w0 = pl.multiple_of(jnp.maximum(q0 + tq - wlen, 0), tq)
    sw = _dot_nt(q_plain, kw_ref[0, 0, pl.ds(w0, wlen), :])
    kp = w0 + lax.broadcasted_iota(jnp.int32, (1, wlen), 1)
    age = lax.bitcast_convert_type(t_rows - kp[:, :tq], jnp.uint32)
    sw = jnp.concatenate([jnp.where(age < WINDOW, sw[:, :tq], -jnp.inf),
                          jnp.where(kp[:, tq:] <= t_rows, sw[:, tq:], -jnp.inf)], axis=1)
    o_win = pv_heads(weights(sw, jnp.max(sw, axis=-1, keepdims=True)),
                     vw_ref[0, 0, pl.ds(w0, wlen), :])

    sg = jax.nn.sigmoid(gl_ref[0, 0])
    rowsum = lambda a: a[:, ROWSUM_LANE:ROWSUM_LANE + 1]
    heads = []
    for r in range(GQA_R):
        g_cmp, g_sel, g_win = (sg[:, 3 * r + j:3 * r + j + 1] for j in range(3))
        o = (g_cmp * o_cmp[r] + (g_sel / rowsum(o_sel[r])) * o_sel[r]
             + (g_win / rowsum(o_win[r])) * o_win[r])
        heads.append(o[:, :HEAD_DIM])
    o_ref[0] = jnp.concatenate(heads, axis=1).astype(BF16)


def _attention(q, kc_aug, vc_p, ks_aug, vs_p, kw_aug, vw_p, gl, ovl):
    b, _, s, _ = q.shape
    nc = kc_aug.shape[2]
    tq = min(TQ, s)
    kv = lambda n: pl.BlockSpec((1, 1, n, LANES), lambda bi, gi, i: (bi, gi, 0, 0))
    return pl.pallas_call(
        _attn_kernel,
        grid=(b, N_KV, s // tq),
        in_specs=[pl.BlockSpec((1, GQA_R, tq, LANES), lambda bi, gi, i: (bi, gi, i, 0)),
                  kv(nc), kv(nc), kv(s), kv(s), kv(s), kv(s),
                  pl.BlockSpec((1, 1, tq, LANES), lambda bi, gi, i: (bi, gi, i, 0)),
                  pl.BlockSpec((N_BLK, nc), lambda bi, gi, i: (0, 0))],
        out_specs=pl.BlockSpec((1, tq, GQA_R * HEAD_DIM), lambda bi, gi, i: (bi, i, gi)),
        out_shape=jax.ShapeDtypeStruct((b, s, NSA_WIDTH), BF16),
        scratch_shapes=[pltpu.VMEM((N_BLK, tq), F32), pltpu.VMEM((N_BLK, tq), F32)],
        compiler_params=_cparams(3),
        name="nsa_attention",
    )(q, kc_aug, vc_p, ks_aug, vs_p, kw_aug, vw_p, gl, ovl)


def _merge_kernel(x_ref, mod_ref, gpre_ref, gpost_ref, on_ref, u_ref, uprev_ref, wbrg_ref,
                  wpool_ref, pscale_ref, wbn_ref, wbp_ref, wout_ref, o_ref):
    tm = x_ref.shape[1]
    i = pl.program_id(1)
    x = x_ref[0]
    h = (_rms(x) * gpre_ref[...]) * (1.0 + mod_ref[0, 1:2, :]) + mod_ref[0, 0:1, :]
    br = jax.nn.sigmoid(_dot(h.astype(BF16), wbrg_ref[...]))

    u = u_ref[0]
    prev = jnp.where(i > 0, uprev_ref[0], 0.0)
    ext = jnp.concatenate([prev, u], axis=0)
    tpos = i * tm + lax.broadcasted_iota(jnp.int32, (tm, 1), 0)
    mixed = []
    for gidx, w in enumerate(POOL_SIZES):
        acc = ext[:, gidx * POOL_GROUP:(gidx + 1) * POOL_GROUP]
        span = 1
        while span < w:
            acc = acc + pltpu.roll(acc, span, 0)
            span *= 2
        cnt = jnp.minimum(tpos + 1, w).astype(F32)
        pooled = acc[POOL_HALO:] / cnt - u[:, gidx * POOL_GROUP:(gidx + 1) * POOL_GROUP]
        mixed.append(_dot(pooled.astype(BF16), wpool_ref[gidx]))
    o_pool = jnp.concatenate(mixed, axis=1) * pscale_ref[...]

    merged = (br[:, :D_MODEL] * _dot(on_ref[0], wbn_ref[...])
              + br[:, D_MODEL:] * _dot(o_pool.astype(BF16), wbp_ref[...]))
    y = _dot(merged.astype(BF16), wout_ref[...])
    o_ref[0] = x + mod_ref[0, 2:3, :] * (_rms(y) * gpost_ref[...])


def _merge(x, mod, g_pre, g_post, o_nsa, u, w_brg, w_pool, pool_scale, w_br_nsa, w_br_pool, w_out):
    b, s, d = x.shape
    tm = min(TM_PROJ, s)
    halo_per_tile = tm // POOL_HALO
    tile = lambda w: pl.BlockSpec((1, tm, w), lambda bi, i: (bi, i, 0))
    c2 = lambda shape: pl.BlockSpec(shape, lambda bi, i: (0, 0))
    return pl.pallas_call(
        _merge_kernel,
        grid=(b, s // tm),
        in_specs=[tile(d),
                  pl.BlockSpec((1, N_MOD, d), lambda bi, i: (bi, 0, 0)),
                  c2((1, d)), c2((1, d)),
                  tile(NSA_WIDTH), tile(POOL_WIDTH),
                  pl.BlockSpec((1, POOL_HALO, POOL_WIDTH),
                               lambda bi, i: (bi, jnp.maximum(i * halo_per_tile - 1, 0), 0)),
                  c2((d, 2 * d)),
                  pl.BlockSpec((len(POOL_SIZES), POOL_GROUP, POOL_GROUP), lambda bi, i: (0, 0, 0)),
                  c2((1, POOL_WIDTH)),
                  c2((NSA_WIDTH, d)), c2((POOL_WIDTH, d)), c2((d, d))],
        out_specs=tile(d),
        out_shape=jax.ShapeDtypeStruct((b, s, d), F32),
        compiler_params=_cparams(2),
        name="merge",
    )(x, mod, g_pre, g_post, o_nsa, u, u, w_brg, w_pool, pool_scale, w_br_nsa, w_br_pool, w_out)


def _mlp_kernel(x_ref, mod_ref, gpre_ref, gpost_ref, w1_ref, w2_ref, o_ref):
    x = x_ref[0]
    h = ((_rms(x) * gpre_ref[...]) * (1.0 + mod_ref[0, 4:5, :]) + mod_ref[0, 3:4, :]).astype(BF16)
    y = jnp.zeros(x.shape, F32)
    for c0 in range(0, D_FF, FF_CHUNK):
        a = jnp.maximum(_dot(h, w1_ref[:, c0:c0 + FF_CHUNK]), 0.0)
        y = y + _dot((a * a).astype(BF16), w2_ref[c0:c0 + FF_CHUNK, :])
    o_ref[0] = x + mod_ref[0, 5:6, :] * (_rms(y) * gpost_ref[...])


def _mlp(x, mod, g_pre, g_post, w_fc1, w_fc2):
    b, s, d = x.shape
    tm = min(TM_PROJ, s)
    tile = pl.BlockSpec((1, tm, d), lambda bi, i: (bi, i, 0))
    resident = lambda shape: pl.BlockSpec(shape, lambda bi, i: (0, 0), pipeline_mode=pl.Buffered(1))
    return pl.pallas_call(
        _mlp_kernel,
        grid=(b, s // tm),
        in_specs=[tile,
                  pl.BlockSpec((1, N_MOD, d), lambda bi, i: (bi, 0, 0)),
                  pl.BlockSpec((1, d), lambda bi, i: (0, 0)),
                  pl.BlockSpec((1, d), lambda bi, i: (0, 0)),
                  resident((d, D_FF)), resident((D_FF, d))],
        out_specs=tile,
        out_shape=jax.ShapeDtypeStruct((b, s, d), F32),
        compiler_params=_cparams(2),
        name="mlp",
    )(x, mod, g_pre, g_post, w_fc1, w_fc2)


def _rope_tables(s):
    half = HEAD_DIM // 2
    inv = ROPE_THETA ** (-jnp.arange(half, dtype=F32) / half)
    ang = jnp.arange(s, dtype=F32)[:, None] * inv[None, :]
    cos = jnp.cos(ang)
    sin = jnp.sin(ang)
    reps = LANES // HEAD_DIM
    cos_t = jnp.tile(jnp.concatenate([cos, cos], axis=1), (1, reps))
    sin_t = jnp.tile(jnp.concatenate([-sin, sin], axis=1), (1, reps))
    return cos_t, sin_t


def _overlap_table(nc, n_sel):
    cmp_start = np.arange(nc) * CMP_STRIDE
    sel_start = np.arange(n_sel) * SEL_BLOCK
    ovl = ((cmp_start[None, :] < sel_start[:, None] + SEL_BLOCK)
           & (cmp_start[None, :] + CMP_BLOCK > sel_start[:, None])).astype(np.float32)
    out = np.zeros((N_BLK, nc), np.float32)
    out[:n_sel] = ovl
    return jnp.asarray(out, dtype=BF16)


def _pack_in_weights(w_in):
    col_scale = jnp.concatenate([jnp.full((NSA_WIDTH,), HEAD_DIM ** -0.5 * LOG2E, F32),
                                 jnp.ones((w_in.shape[1] - NSA_WIDTH,), F32)])
    w = (w_in * col_scale).astype(BF16)
    q, kc, vc, ks, vs, kw, vw, gate, u, brg = jnp.split(w, IN_SPLITS, axis=-1)
    d = w_in.shape[0]
    gate = gate.reshape(d, N_KV, GQA_R * 3)
    gate = jnp.pad(gate, ((0, 0), (0, 0), (0, HEAD_DIM - GQA_R * 3))).reshape(d, KV_WIDTH)
    return jnp.concatenate([q, kc, ks, kw, vc, vs, vw, gate, u], axis=1), brg


def kernel(x, c, w_ada, b_ada, g_pre_mix, g_post_mix, w_in, cmp_pe_k, cmp_w1_k, cmp_w2_k,
           cmp_pe_v, cmp_w1_v, cmp_w2_v, w_pool, pool_scale, w_br_nsa, w_br_pool, w_out,
           g_pre_mlp, g_post_mlp, w_fc1, w_fc2):
    b, s, d = x.shape
    depth = w_ada.shape[0]
    n_sel = s // SEL_BLOCK
    nc = s // CMP_STRIDE
    assert d == D_MODEL and s % TM_PROJ == 0 and s >= WINDOW + TQ
    assert SEL_TOPK <= n_sel <= N_BLK
    cos_t, sin_t = _rope_tables(s)
    ovl = _overlap_table(nc, n_sel)

    for l in range(depth):
        mod = _modulation(c, w_ada, b_ada, l).reshape(b, N_MOD, d)
        w_cat, w_brg = _pack_in_weights(w_in[l])
        q, kc, vc, ks_aug, kw_aug, vs_p, vw_p, gl, u = _inproj(
            x, mod, g_pre_mix[l].reshape(1, d), w_cat, cos_t, sin_t)
        pek, w1k, w2k = _compress_weights(cmp_pe_k[l], cmp_w1_k[l], cmp_w2_k[l])
        pev, w1v, w2v = _compress_weights(cmp_pe_v[l], cmp_w1_v[l], cmp_w2_v[l])
        kc_aug, vc_p = _compress(kc, vc, pek, pev, w1k, w1v, w2k, w2v)
        o_nsa = _attention(q, kc_aug, vc_p, ks_aug, vs_p, kw_aug, vw_p, gl, ovl)
        x = _merge(x, mod, g_pre_mix[l].reshape(1, d), g_post_mix[l].reshape(1, d), o_nsa, u,
                   w_brg, w_pool[l].astype(BF16), pool_scale[l].reshape(1, POOL_WIDTH),
                   w_br_nsa[l].astype(BF16), w_br_pool[l].astype(BF16), w_out[l].astype(BF16))
        x = _mlp(x, mod, g_pre_mlp[l].reshape(1, d), g_post_mlp[l].reshape(1, d),
                 w_fc1[l].astype(BF16), w_fc2[l].astype(BF16))
    return x
```

```python
import functools

import numpy as np
import jax
import jax.numpy as jnp
from jax import lax
from jax.experimental import pallas as pl
from jax.experimental.pallas import tpu as pltpu

F32 = jnp.float32
BF16 = jnp.bfloat16

D_MODEL = 1024
N_HEADS = 8
N_KV = 2
HEAD_DIM = 64
GQA_R = N_HEADS // N_KV
NSA_WIDTH = N_HEADS * HEAD_DIM
KV_WIDTH = N_KV * HEAD_DIM
CMP_BLOCK = 32
CMP_STRIDE = 16
SEL_BLOCK = 64
SEL_TOPK = 16
N_FORCED_LOCAL = 2
WINDOW = 512
FORCE_SCORE = 1e9
POOL_SIZES = (2, 4, 8, 16)
POOL_GROUP = 128
POOL_WIDTH = POOL_GROUP * len(POOL_SIZES)
D_FF = 4 * D_MODEL
ROPE_THETA = 10000.0
EPS = 1e-6
N_MOD = 6
IN_SIZES = (NSA_WIDTH, KV_WIDTH, KV_WIDTH, KV_WIDTH, KV_WIDTH, KV_WIDTH, KV_WIDTH,
            3 * N_HEADS, POOL_WIDTH, 2 * D_MODEL)
IN_SPLITS = [int(v) for v in np.cumsum(IN_SIZES)[:-1]]

LANES = 128
SUBLANES = 8
N_BLK = LANES - HEAD_DIM
ROWSUM_LANE = HEAD_DIM
LOG2E = 1.4426950408889634
POOL_HALO = 16
BLOCK_MASK_BIAS = -float(2 ** 30)
NOT_CAUSAL_SCORE = -1e30

C_Q = 0
C_KC = C_Q + NSA_WIDTH
C_KS = C_KC + KV_WIDTH
C_KW = C_KS + KV_WIDTH
C_VC = C_KW + KV_WIDTH
C_VS = C_VC + KV_WIDTH
C_VW = C_VS + KV_WIDTH
C_GATE = C_VW + KV_WIDTH
C_U = C_GATE + LANES
IN_COLS = C_U + POOL_WIDTH

TM_PROJ = 512
TQ = 256
TK_SEL = 1024
TK_SEL_DIAG = 512
FF_CHUNK = 1024
VMEM_LIMIT = 56 * 1024 * 1024


def _cparams(n_axes):
    return pltpu.CompilerParams(dimension_semantics=("arbitrary",) * n_axes,
                                vmem_limit_bytes=VMEM_LIMIT)


def _dot(a, b):
    return jnp.dot(a, b, preferred_element_type=F32)


def _dot_nt(a, b):
    return lax.dot_general(a, b, (((1,), (1,)), ((), ())), preferred_element_type=F32)


def _rms(x):
    return x * lax.rsqrt(jnp.mean(x * x, axis=-1, keepdims=True) + EPS)


def _mod_kernel(c_ref, w_ref, b_ref, o_ref):
    c = c_ref[...]
    a = c * jax.nn.sigmoid(c)
    o_ref[...] = jnp.dot(a, w_ref[...], preferred_element_type=F32,
                         precision=lax.Precision.HIGHEST) + b_ref[...]


def _modulation(c, w_ada, b_ada, layer):
    b, d = c.shape
    n = w_ada.shape[2]
    tn = 2048
    return pl.pallas_call(
        _mod_kernel,
        grid=(n // tn,),
        in_specs=[pl.BlockSpec((b, d), lambda j: (0, 0)),
                  pl.BlockSpec((None, d, tn), lambda j: (layer, 0, j)),
                  pl.BlockSpec((None, 1, tn), lambda j: (layer, 0, j))],
        out_specs=pl.BlockSpec((b, tn), lambda j: (0, j)),
        out_shape=jax.ShapeDtypeStruct((b, n), F32),
        compiler_params=_cparams(1),
        name="modulation",
    )(c, w_ada, b_ada.reshape(-1, 1, n))


def _inproj_kernel(x_ref, mod_ref, g_ref, w_ref, cos_ref, sin_ref,
                   q_ref, kc_ref, vc_ref, ks_ref, kw_ref, vs_ref, vw_ref, gl_ref, u_ref):
    tm = x_ref.shape[1]
    h = (_rms(x_ref[0]) * g_ref[...]) * (1.0 + mod_ref[0, 1:2, :]) + mod_ref[0, 0:1, :]
    p = _dot(h.astype(BF16), w_ref[...])

    lane = lax.broadcasted_iota(jnp.int32, (tm, LANES), 1)
    lo = lane < HEAD_DIM
    first_half = (lane % HEAD_DIM) < (HEAD_DIM // 2)
    cos = cos_ref[...]
    sin = sin_ref[...]

    def rope(v):
        rot = jnp.where(first_half, pltpu.roll(v, LANES - HEAD_DIM // 2, 1),
                        pltpu.roll(v, HEAD_DIM // 2, 1))
        return v * cos + rot * sin

    def swap(v):
        return pltpu.roll(v, HEAD_DIM, 1)

    def slab(c0):
        return p[:, c0:c0 + LANES]

    for j in range(N_HEADS // 2):
        v = rope(slab(C_Q + j * LANES))
        q_ref[0, 2 * j] = jnp.where(lo, v, 0.0).astype(BF16)
        q_ref[0, 2 * j + 1] = jnp.where(lo, swap(v), 0.0).astype(BF16)

    kc_ref[0] = rope(slab(C_KC))
    vc_ref[0] = slab(C_VC)

    pos = pl.program_id(1) * tm + lax.broadcasted_iota(jnp.int32, (tm, LANES), 0)
    onehot = jnp.where(lane - HEAD_DIM == pos // SEL_BLOCK, 1.0, 0.0)
    v = rope(slab(C_KS))
    ks_ref[0, 0] = jnp.where(lo, v, onehot).astype(BF16)
    ks_ref[0, 1] = jnp.where(lo, swap(v), onehot).astype(BF16)
    v = rope(slab(C_KW))
    kw_ref[0, 0] = jnp.where(lo, v, 0.0).astype(BF16)
    kw_ref[0, 1] = jnp.where(lo, swap(v), 0.0).astype(BF16)

    ones_col = jnp.where(lane == ROWSUM_LANE, 1.0, 0.0)
    for c0, ref in ((C_VS, vs_ref), (C_VW, vw_ref)):
        v = slab(c0)
        ref[0, 0] = jnp.where(lo, v, ones_col).astype(BF16)
        ref[0, 1] = jnp.where(lo, swap(v), ones_col).astype(BF16)

    gates = slab(C_GATE)
    gl_ref[0, 0] = gates
    gl_ref[0, 1] = swap(gates)
    u_ref[0] = p[:, C_U:C_U + POOL_WIDTH]


def _inproj(x, mod, g_pre, w_cat, cos_t, sin_t):
    b, s, d = x.shape
    tm = min(TM_PROJ, s)
    nt = s // tm
    const = lambda bi, i: (0, 0)
    out_shape = (
        jax.ShapeDtypeStruct((b, N_HEADS, s, LANES), BF16),
        jax.ShapeDtypeStruct((b, s, KV_WIDTH), F32),
        jax.ShapeDtypeStruct((b, s, KV_WIDTH), F32),
        jax.ShapeDtypeStruct((b, N_KV, s, LANES), BF16),
        jax.ShapeDtypeStruct((b, N_KV, s, LANES), BF16),
        jax.ShapeDtypeStruct((b, N_KV, s, LANES), BF16),
        jax.ShapeDtypeStruct((b, N_KV, s, LANES), BF16),
        jax.ShapeDtypeStruct((b, N_KV, s, LANES), F32),
        jax.ShapeDtypeStruct((b, s, POOL_WIDTH), F32),
    )
    kv4 = lambda w: pl.BlockSpec((1, N_KV, tm, w), lambda bi, i: (bi, 0, i, 0))
    tok = lambda w: pl.BlockSpec((1, tm, w), lambda bi, i: (bi, i, 0))
    out_specs = (
        pl.BlockSpec((1, N_HEADS, tm, LANES), lambda bi, i: (bi, 0, i, 0)),
        tok(KV_WIDTH), tok(KV_WIDTH), kv4(LANES), kv4(LANES), kv4(LANES), kv4(LANES), kv4(LANES),
        tok(POOL_WIDTH),
    )
    return pl.pallas_call(
        _inproj_kernel,
        grid=(b, nt),
        in_specs=[pl.BlockSpec((1, tm, d), lambda bi, i: (bi, i, 0)),
                  pl.BlockSpec((1, N_MOD, d), lambda bi, i: (bi, 0, 0)),
                  pl.BlockSpec((1, d), const),
                  pl.BlockSpec((d, IN_COLS), const),
                  pl.BlockSpec((tm, LANES), lambda bi, i: (i, 0)),
                  pl.BlockSpec((tm, LANES), lambda bi, i: (i, 0))],
        out_specs=out_specs,
        out_shape=out_shape,
        compiler_params=_cparams(2),
        name="inproj",
    )(x, mod, g_pre, w_cat, cos_t, sin_t)


def _gelu_tanh(x):
    return 0.5 * x * (1.0 + jnp.tanh(np.sqrt(2.0 / np.pi) * (x + 0.044715 * (x * x * x))))


def _compress_kernel(k_ref, v_ref, pek_ref, pev_ref, w1k_ref, w1v_ref, w2k_ref, w2v_ref,
                     kc_ref, vc_ref):
    nc = kc_ref.shape[2]
    lane = lax.broadcasted_iota(jnp.int32, (nc, LANES), 1)
    lo = lane < HEAD_DIM

    def run(x_ref, pe_ref, w1_ref, w2_ref):
        chunks = jnp.concatenate(
            [x_ref[0, pl.ds(l, nc, stride=CMP_STRIDE), :] for l in range(CMP_STRIDE)], axis=1)
        pa = _dot((chunks + pe_ref[0:1, :]).astype(BF16), w1_ref[0])
        pb = _dot((chunks + pe_ref[1:2, :]).astype(BF16), w1_ref[1])
        pb = jnp.concatenate([pb[1:], jnp.zeros((1, LANES), F32)], axis=0)
        hid = _gelu_tanh(pa + pb)
        out = _dot(hid.astype(BF16), w2_ref[...])
        return (jnp.where(lo, out, 0.0).astype(BF16),
                jnp.where(lo, pltpu.roll(out, HEAD_DIM, 1), 0.0).astype(BF16))

    kc_ref[0, 0], kc_ref[0, 1] = run(k_ref, pek_ref, w1k_ref, w2k_ref)
    vc_ref[0, 0], vc_ref[0, 1] = run(v_ref, pev_ref, w1v_ref, w2v_ref)


def _compress_weights(pe, w1, w2):
    eye = jnp.eye(N_KV, dtype=F32)
    pe2 = jnp.tile(pe.reshape(2, CMP_STRIDE, 1, HEAD_DIM), (1, 1, N_KV, 1))
    pe2 = pe2.reshape(2, CMP_STRIDE * KV_WIDTH)
    w1r = w1.reshape(2, CMP_STRIDE, HEAD_DIM, HEAD_DIM)
    w1b = jnp.einsum("hlde,gk->hlgdke", w1r, eye).reshape(2, CMP_STRIDE * KV_WIDTH, KV_WIDTH)
    w2b = jnp.einsum("de,gk->gdke", w2, eye).reshape(KV_WIDTH, KV_WIDTH)
    return pe2, w1b.astype(BF16), w2b.astype(BF16)


def _compress(kc, vc, pek, pev, w1k, w1v, w2k, w2v):
    b, s, width = kc.shape
    nc = s // CMP_STRIDE
    wide = CMP_STRIDE * width
    blk = pl.BlockSpec((1, s, width), lambda bi: (bi, 0, 0))
    c2 = lambda bi: (0, 0)
    c3 = lambda bi: (0, 0, 0)
    out = pl.BlockSpec((1, N_KV, nc, LANES), lambda bi: (bi, 0, 0, 0))
    return pl.pallas_call(
        _compress_kernel,
        grid=(b,),
        in_specs=[blk, blk,
                  pl.BlockSpec((2, wide), c2), pl.BlockSpec((2, wide), c2),
                  pl.BlockSpec((2, wide, width), c3), pl.BlockSpec((2, wide, width), c3),
                  pl.BlockSpec((width, width), c2), pl.BlockSpec((width, width), c2)],
        out_specs=(out, out),
        out_shape=(jax.ShapeDtypeStruct((b, N_KV, nc, LANES), BF16),
                   jax.ShapeDtypeStruct((b, N_KV, nc, LANES), BF16)),
        compiler_params=_cparams(1),
        name="compress",
    )(kc, vc, pek, pev, w1k, w1v, w2k, w2v)


def _attn_kernel(q_ref, kc_ref, vc_ref, ks_ref, vs_ref, kw_ref, vw_ref, gl_ref, ovl_ref, o_ref,
                 val_ref, rank_ref, m_ref, acc_ref):
    tq = q_ref.shape[2]
    nc = kc_ref.shape[2]
    s_len = ks_ref.shape[2]
    rows = GQA_R * tq
    q0 = pl.program_id(2) * tq

    q4 = q_ref[0]
    t_col = q0 + lax.broadcasted_iota(jnp.int32, (tq, 1), 0)
    t_rows = jnp.concatenate([t_col] * GQA_R, axis=0)

    def head(a, r):
        return a[r * tq:(r + 1) * tq]

    def pv_heads(pb, v):
        return [_dot(head(pb, r), v) for r in range(GQA_R)]

    def weights(s, m):
        return jnp.exp2((s - m).astype(BF16))

    q_plain = q4.reshape(rows, LANES)
    sc = _dot_nt(q_plain, kc_ref[0, 0])
    cmp_end = lax.broadcasted_iota(jnp.int32, (1, nc), 1) * CMP_STRIDE + (CMP_BLOCK - 1)
    sc = jnp.where(cmp_end <= t_rows, sc, -jnp.inf)
    m = jnp.max(sc, axis=-1, keepdims=True)
    m = jnp.where(m == -jnp.inf, 0.0, m)
    e = jnp.exp2(sc - m)
    p_cmp = e / jnp.maximum(jnp.sum(e, axis=-1, keepdims=True), 1e-30)
    o_cmp = pv_heads(p_cmp.astype(BF16), vc_ref[0, 0])

    p_sum = p_cmp[0:tq] + p_cmp[tq:2 * tq] + p_cmp[2 * tq:3 * tq] + p_cmp[3 * tq:]
    p_hi = p_sum.astype(BF16)
    rem = p_sum - p_hi.astype(F32)
    p_mid = rem.astype(BF16)
    p_lo = (rem - p_mid.astype(F32)).astype(BF16)
    ovl_t = ovl_ref[...]
    imp = _dot_nt(ovl_t, p_hi) + _dot_nt(ovl_t, p_mid) + _dot_nt(ovl_t, p_lo)
    blk = lax.broadcasted_iota(jnp.int32, (N_BLK, tq), 0)
    cur = (q0 + lax.broadcasted_iota(jnp.int32, (1, tq), 1)) // SEL_BLOCK
    causal = blk <= cur
    forced = (blk == 0) | (causal & (blk > cur - N_FORCED_LOCAL))
    val = jnp.where(forced, FORCE_SCORE, jnp.where(causal, imp, NOT_CAUSAL_SCORE))

    n_tiles = N_BLK // SUBLANES
    row8 = lax.broadcasted_iota(jnp.int32, (SUBLANES, tq), 0)
    val_ref[...] = val
    rank_ref[...] = jnp.zeros((N_BLK, tq), F32)
    last_causal = (q0 + tq - 1) // SEL_BLOCK
    for src in range(n_tiles):
        @pl.when(src * SUBLANES <= last_causal)
        def _(src=src):
            rows_b = [jnp.broadcast_to(val_ref[mp:mp + 1, :], (SUBLANES, tq))
                      for mp in range(src * SUBLANES, (src + 1) * SUBLANES)]
            for dst in range(n_tiles):
                sl = slice(dst * SUBLANES, (dst + 1) * SUBLANES)
                v = val_ref[sl, :]
                acc = jnp.zeros((SUBLANES, tq), F32)
                for i, b in enumerate(rows_b):
                    if src < dst:
                        inc = jnp.where(b >= v, 1.0, 0.0)
                    elif src > dst:
                        inc = jnp.where(b > v, 1.0, 0.0)
                    else:
                        inc = jnp.where(row8 > i, jnp.where(b >= v, 1.0, 0.0),
                                        jnp.where(b > v, 1.0, 0.0))
                    acc = acc + inc
                rank_ref[sl, :] += acc
    rank = rank_ref[...]
    bias_t = jnp.where(rank < float(SEL_TOPK), 0.0, BLOCK_MASK_BIAS)
    bias = jnp.concatenate([jnp.zeros((HEAD_DIM, tq), F32), bias_t], axis=0).T
    q_aug = (q4 + bias.astype(BF16)[None]).reshape(rows, LANES)

    tk_wide = min(TK_SEL, s_len)

    def sel_step(j, _, base, tk, diagonal):
        k0 = pl.multiple_of(base + j * tk, tk)
        s = _dot_nt(q_aug, ks_ref[0, 0, pl.ds(k0, tk), :])
        if diagonal:
            kpos = k0 + lax.broadcasted_iota(jnp.int32, (1, tk), 1)
            s = jnp.where(kpos <= t_rows, s, -jnp.inf)
        m_i = m_ref[...]
        m_new = jnp.maximum(m_i, jnp.max(s, axis=-1, keepdims=True))
        m_ref[...] = m_new
        alpha = jnp.exp2(m_i - m_new)
        pb = jnp.exp2((s - pltpu.repeat(m_new, tk // LANES, axis=1)).astype(BF16))
        outs = pv_heads(pb, vs_ref[0, 0, pl.ds(k0, tk), :])
        for r in range(GQA_R):
            acc_ref[r] = head(alpha, r) * acc_ref[r] + outs[r]
        return 0

    tk_diag = min(TK_SEL_DIAG, s_len)
    n_wide = q0 // tk_wide
    diag0 = (q0 // tk_diag) * tk_diag
    n_mid = (diag0 - n_wide * tk_wide) // tk_diag
    m_ref[...] = jnp.full((rows, LANES), -jnp.inf, F32)
    acc_ref[...] = jnp.zeros((GQA_R, tq, LANES), F32)
    lax.fori_loop(0, n_wide, functools.partial(sel_step, base=0, tk=tk_wide, diagonal=False), 0)
    lax.fori_loop(0, n_mid,
                  functools.partial(sel_step, base=n_wide * tk_wide, tk=tk_diag, diagonal=False), 0)
    sel_step(0, 0, base=diag0, tk=tk_diag, diagonal=True)
    o_sel = [acc_ref[r] for r in range(GQA_R)]

    wlen = min(WINDOW + tq, s_len)
    w0 = pl.multiple_of(jnp.maximum(q0 + tq - wlen, 0), tq)
    sw = _dot_nt(q_plain, kw_ref[0, 0, pl.ds(w0, wlen), :])
    kp = w0 + lax.broadcasted_iota(jnp.int32, (1, wlen), 1)
    age = lax.bitcast_convert_type(t_rows - kp[:, :tq], jnp.uint32)
    sw = jnp.concatenate([jnp.where(age < WINDOW, sw[:, :tq], -jnp.inf),
                          jnp.where(kp[:, tq:] <= t_rows, sw[:, tq:], -jnp.inf)], axis=1)
    o_win = pv_heads(weights(sw, jnp.max(sw, axis=-1, keepdims=True)),
                     vw_ref[0, 0, pl.ds(w0, wlen), :])

    sg = jax.nn.sigmoid(gl_ref[0, 0])
    rowsum = lambda a: a[:, ROWSUM_LANE:ROWSUM_LANE + 1]
    heads = []
    for r in range(GQA_R):
        g_cmp, g_sel, g_win = (sg[:, 3 * r + j:3 * r + j + 1] for j in range(3))
        o = (g_cmp * o_cmp[r] + (g_sel / rowsum(o_sel[r])) * o_sel[r]
             + (g_win / rowsum(o_win[r])) * o_win[r])
        heads.append(o[:, :HEAD_DIM])
    o_ref[0] = jnp.concatenate(heads, axis=1).astype(BF16)


def _attention(q, kc_aug, vc_p, ks_aug, vs_p, kw_aug, vw_p, gl, ovl):
    b, _, s, _ = q.shape
    nc = kc_aug.shape[2]
    tq = min(TQ, s)
    kv = lambda n: pl.BlockSpec((1, 1, n, LANES), lambda bi, gi, i: (bi, gi, 0, 0))
    return pl.pallas_call(
        _attn_kernel,
        grid=(b, N_KV, s // tq),
        in_specs=[pl.BlockSpec((1, GQA_R, tq, LANES), lambda bi, gi, i: (bi, gi, i, 0)),
                  kv(nc), kv(nc), kv(s), kv(s), kv(s), kv(s),
                  pl.BlockSpec((1, 1, tq, LANES), lambda bi, gi, i: (bi, gi, i, 0)),
                  pl.BlockSpec((N_BLK, nc), lambda bi, gi, i: (0, 0))],
        out_specs=pl.BlockSpec((1, tq, GQA_R * HEAD_DIM), lambda bi, gi, i: (bi, i, gi)),
        out_shape=jax.ShapeDtypeStruct((b, s, NSA_WIDTH), BF16),
        scratch_shapes=[pltpu.VMEM((N_BLK, tq), F32), pltpu.VMEM((N_BLK, tq), F32),
                        pltpu.VMEM((GQA_R * tq, LANES), F32), pltpu.VMEM((GQA_R, tq, LANES), F32)],
        compiler_params=_cparams(3),
        name="nsa_attention",
    )(q, kc_aug, vc_p, ks_aug, vs_p, kw_aug, vw_p, gl, ovl)


def _merge_kernel(x_ref, mod_ref, gpre_ref, gpost_ref, on_ref, u_ref, uprev_ref, wbrg_ref,
                  wpool_ref, pscale_ref, wbn_ref, wbp_ref, wout_ref, o_ref):
    tm = x_ref.shape[1]
    i = pl.program_id(1)
    x = x_ref[0]
    h = (_rms(x) * gpre_ref[...]) * (1.0 + mod_ref[0, 1:2, :]) + mod_ref[0, 0:1, :]
    br = jax.nn.sigmoid(_dot(h.astype(BF16), wbrg_ref[...]))

    u = u_ref[0]
    prev = jnp.where(i > 0, uprev_ref[0], 0.0)
    ext = jnp.concatenate([prev, u], axis=0)
    tpos = i * tm + lax.broadcasted_iota(jnp.int32, (tm, 1), 0)
    mixed = []
    for gidx, w in enumerate(POOL_SIZES):
        acc = ext[:, gidx * POOL_GROUP:(gidx + 1) * POOL_GROUP]
        span = 1
        while span < w:
            acc = acc + pltpu.roll(acc, span, 0)
            span *= 2
        cnt = jnp.minimum(tpos + 1, w).astype(F32)
        pooled = acc[POOL_HALO:] / cnt - u[:, gidx * POOL_GROUP:(gidx + 1) * POOL_GROUP]
        mixed.append(_dot(pooled.astype(BF16), wpool_ref[gidx]))
    o_pool = jnp.concatenate(mixed, axis=1) * pscale_ref[...]

    merged = (br[:, :D_MODEL] * _dot(on_ref[0], wbn_ref[...])
              + br[:, D_MODEL:] * _dot(o_pool.astype(BF16), wbp_ref[...]))
    y = _dot(merged.astype(BF16), wout_ref[...])
    o_ref[0] = x + mod_ref[0, 2:3, :] * (_rms(y) * gpost_ref[...])


def _merge(x, mod, g_pre, g_post, o_nsa, u, w_brg, w_pool, pool_scale, w_br_nsa, w_br_pool, w_out):
    b, s, d = x.shape
    tm = min(TM_PROJ, s)
    halo_per_tile = tm // POOL_HALO
    tile = lambda w: pl.BlockSpec((1, tm, w), lambda bi, i: (bi, i, 0))
    c2 = lambda shape: pl.BlockSpec(shape, lambda bi, i: (0, 0))
    return pl.pallas_call(
        _merge_kernel,
        grid=(b, s // tm),
        in_specs=[tile(d),
                  pl.BlockSpec((1, N_MOD, d), lambda bi, i: (bi, 0, 0)),
                  c2((1, d)), c2((1, d)),
                  tile(NSA_WIDTH), tile(POOL_WIDTH),
                  pl.BlockSpec((1, POOL_HALO, POOL_WIDTH),
                               lambda bi, i: (bi, jnp.maximum(i * halo_per_tile - 1, 0), 0)),
                  c2((d, 2 * d)),
                  pl.BlockSpec((len(POOL_SIZES), POOL_GROUP, POOL_GROUP), lambda bi, i: (0, 0, 0)),
                  c2((1, POOL_WIDTH)),
                  c2((NSA_WIDTH, d)), c2((POOL_WIDTH, d)), c2((d, d))],
        out_specs=tile(d),
        out_shape=jax.ShapeDtypeStruct((b, s, d), F32),
        compiler_params=_cparams(2),
        name="merge",
    )(x, mod, g_pre, g_post, o_nsa, u, u, w_brg, w_pool, pool_scale, w_br_nsa, w_br_pool, w_out)


def _mlp_kernel(x_ref, mod_ref, gpre_ref, gpost_ref, w1_ref, w2_ref, o_ref):
    x = x_ref[0]
    h = ((_rms(x) * gpre_ref[...]) * (1.0 + mod_ref[0, 4:5, :]) + mod_ref[0, 3:4, :]).astype(BF16)
    y = jnp.zeros(x.shape, F32)
    for c0 in range(0, D_FF, FF_CHUNK):
        a = jnp.maximum(_dot(h, w1_ref[:, c0:c0 + FF_CHUNK]), 0.0)
        y = y + _dot((a * a).astype(BF16), w2_ref[c0:c0 + FF_CHUNK, :])
    o_ref[0] = x + mod_ref[0, 5:6, :] * (_rms(y) * gpost_ref[...])


def _mlp(x, mod, g_pre, g_post, w_fc1, w_fc2):
    b, s, d = x.shape
    tm = min(TM_PROJ, s)
    tile = pl.BlockSpec((1, tm, d), lambda bi, i: (bi, i, 0))
    resident = lambda shape: pl.BlockSpec(shape, lambda bi, i: (0, 0), pipeline_mode=pl.Buffered(1))
    return pl.pallas_call(
        _mlp_kernel,
        grid=(b, s // tm),
        in_specs=[tile,
                  pl.BlockSpec((1, N_MOD, d), lambda bi, i: (bi, 0, 0)),
                  pl.BlockSpec((1, d), lambda bi, i: (0, 0)),
                  pl.BlockSpec((1, d), lambda bi, i: (0, 0)),
                  resident((d, D_FF)), resident((D_FF, d))],
        out_specs=tile,
        out_shape=jax.ShapeDtypeStruct((b, s, d), F32),
        compiler_params=_cparams(2),
        name="mlp",
    )(x, mod, g_pre, g_post, w_fc1, w_fc2)


def _rope_tables(s):
    half = HEAD_DIM // 2
    inv = ROPE_THETA ** (-jnp.arange(half, dtype=F32) / half)
    ang = jnp.arange(s, dtype=F32)[:, None] * inv[None, :]
    cos = jnp.cos(ang)
    sin = jnp.sin(ang)
    reps = LANES // HEAD_DIM
    cos_t = jnp.tile(jnp.concatenate([cos, cos], axis=1), (1, reps))
    sin_t = jnp.tile(jnp.concatenate([-sin, sin], axis=1), (1, reps))
    return cos_t, sin_t


def _overlap_table(nc, n_sel):
    cmp_start = np.arange(nc) * CMP_STRIDE
    sel_start = np.arange(n_sel) * SEL_BLOCK
    ovl = ((cmp_start[None, :] < sel_start[:, None] + SEL_BLOCK)
           & (cmp_start[None, :] + CMP_BLOCK > sel_start[:, None])).astype(np.float32)
    out = np.zeros((N_BLK, nc), np.float32)
    out[:n_sel] = ovl
    return jnp.asarray(out, dtype=BF16)


def _pack_in_weights(w_in):
    col_scale = jnp.concatenate([jnp.full((NSA_WIDTH,), HEAD_DIM ** -0.5 * LOG2E, F32),
                                 jnp.ones((w_in.shape[1] - NSA_WIDTH,), F32)])
    w = (w_in * col_scale).astype(BF16)
    q, kc, vc, ks, vs, kw, vw, gate, u, brg = jnp.split(w, IN_SPLITS, axis=-1)
    d = w_in.shape[0]
    gate = gate.reshape(d, N_KV, GQA_R * 3)
    gate = jnp.pad(gate, ((0, 0), (0, 0), (0, HEAD_DIM - GQA_R * 3))).reshape(d, KV_WIDTH)
    return jnp.concatenate([q, kc, ks, kw, vc, vs, vw, gate, u], axis=1), brg


def kernel(x, c, w_ada, b_ada, g_pre_mix, g_post_mix, w_in, cmp_pe_k, cmp_w1_k, cmp_w2_k,
           cmp_pe_v, cmp_w1_v, cmp_w2_v, w_pool, pool_scale, w_br_nsa, w_br_pool, w_out,
           g_pre_mlp, g_post_mlp, w_fc1, w_fc2):
    b, s, d = x.shape
    depth = w_ada.shape[0]
    n_sel = s // SEL_BLOCK
    nc = s // CMP_STRIDE
    assert d == D_MODEL and s % TM_PROJ == 0 and s >= WINDOW + TQ
    assert SEL_TOPK <= n_sel <= N_BLK
    cos_t, sin_t = _rope_tables(s)
    ovl = _overlap_table(nc, n_sel)

    for l in range(depth):
        mod = _modulation(c, w_ada, b_ada, l).reshape(b, N_MOD, d)
        w_cat, w_brg = _pack_in_weights(w_in[l])
        q, kc, vc, ks_aug, kw_aug, vs_p, vw_p, gl, u = _inproj(
            x, mod, g_pre_mix[l].reshape(1, d), w_cat, cos_t, sin_t)
        pek, w1k, w2k = _compress_weights(cmp_pe_k[l], cmp_w1_k[l], cmp_w2_k[l])
        pev, w1v, w2v = _compress_weights(cmp_pe_v[l], cmp_w1_v[l], cmp_w2_v[l])
        kc_aug, vc_p = _compress(kc, vc, pek, pev, w1k, w1v, w2k, w2v)
        o_nsa = _attention(q, kc_aug, vc_p, ks_aug, vs_p, kw_aug, vw_p, gl, ovl)
        x = _merge(x, mod, g_pre_mix[l].reshape(1, d), g_post_mix[l].reshape(1, d), o_nsa, u,
                   w_brg, w_pool[l].astype(BF16), pool_scale[l].reshape(1, POOL_WIDTH),
                   w_br_nsa[l].astype(BF16), w_br_pool[l].astype(BF16), w_out[l].astype(BF16))
        x = _mlp(x, mod, g_pre_mlp[l].reshape(1, d), g_post_mlp[l].reshape(1, d),
                 w_fc1[l].astype(BF16), w_fc2[l].astype(BF16))
    return x
```

```python
import functools

import numpy as np
import jax
import jax.numpy as jnp
from jax import lax
from jax.experimental import pallas as pl
from jax.experimental.pallas import tpu as pltpu

F32 = jnp.float32
BF16 = jnp.bfloat16

D_MODEL = 1024
N_HEADS = 8
N_KV = 2
HEAD_DIM = 64
GQA_R = N_HEADS // N_KV
NSA_WIDTH = N_HEADS * HEAD_DIM
KV_WIDTH = N_KV * HEAD_DIM
CMP_BLOCK = 32
CMP_STRIDE = 16
SEL_BLOCK = 64
SEL_TOPK = 16
N_FORCED_LOCAL = 2
WINDOW = 512
FORCE_SCORE = 1e9
POOL_SIZES = (2, 4, 8, 16)
POOL_GROUP = 128
POOL_WIDTH = POOL_GROUP * len(POOL_SIZES)
D_FF = 4 * D_MODEL
ROPE_THETA = 10000.0
EPS = 1e-6
N_MOD = 6
IN_SIZES = (NSA_WIDTH, KV_WIDTH, KV_WIDTH, KV_WIDTH, KV_WIDTH, KV_WIDTH, KV_WIDTH,
            3 * N_HEADS, POOL_WIDTH, 2 * D_MODEL)
IN_SPLITS = [int(v) for v in np.cumsum(IN_SIZES)[:-1]]

LANES = 128
SUBLANES = 8
N_BLK = LANES - HEAD_DIM
ROWSUM_LANE = HEAD_DIM
LOG2E = 1.4426950408889634
POOL_HALO = 16
BLOCK_MASK_BIAS = -float(2 ** 30)
NOT_CAUSAL_SCORE = -1e30

C_Q = 0
C_KC = C_Q + NSA_WIDTH
C_KS = C_KC + KV_WIDTH
C_KW = C_KS + KV_WIDTH
C_VC = C_KW + KV_WIDTH
C_VS = C_VC + KV_WIDTH
C_VW = C_VS + KV_WIDTH
C_GATE = C_VW + KV_WIDTH
C_U = C_GATE + LANES
IN_COLS = C_U + POOL_WIDTH

TM_PROJ = 512
TQ = 256
SEL_TILES = (2048, 1024, 512)
FF_CHUNK = 1024
VMEM_LIMIT = 56 * 1024 * 1024


def _cparams(n_axes):
    return pltpu.CompilerParams(dimension_semantics=("arbitrary",) * n_axes,
                                vmem_limit_bytes=VMEM_LIMIT)


def _dot(a, b):
    return jnp.dot(a, b, preferred_element_type=F32)


def _dot_nt(a, b):
    return lax.dot_general(a, b, (((1,), (1,)), ((), ())), preferred_element_type=F32)


def _rms(x):
    return x * lax.rsqrt(jnp.mean(x * x, axis=-1, keepdims=True) + EPS)


def _mod_kernel(c_ref, w_ref, b_ref, o_ref):
    c = c_ref[...]
    a = c * jax.nn.sigmoid(c)
    o_ref[...] = jnp.dot(a, w_ref[...], preferred_element_type=F32,
                         precision=lax.Precision.HIGHEST) + b_ref[...]


def _modulation(c, w_ada, b_ada, layer):
    b, d = c.shape
    n = w_ada.shape[2]
    tn = 2048
    return pl.pallas_call(
        _mod_kernel,
        grid=(n // tn,),
        in_specs=[pl.BlockSpec((b, d), lambda j: (0, 0)),
                  pl.BlockSpec((None, d, tn), lambda j: (layer, 0, j)),
                  pl.BlockSpec((None, 1, tn), lambda j: (layer, 0, j))],
        out_specs=pl.BlockSpec((b, tn), lambda j: (0, j)),
        out_shape=jax.ShapeDtypeStruct((b, n), F32),
        compiler_params=_cparams(1),
        name="modulation",
    )(c, w_ada, b_ada.reshape(-1, 1, n))


def _inproj_kernel(x_ref, mod_ref, g_ref, w_ref, cos_ref, sin_ref,
                   q_ref, kc_ref, vc_ref, ks_ref, kw_ref, vs_ref, vw_ref, gl_ref, u_ref):
    tm = x_ref.shape[1]
    h = (_rms(x_ref[0]) * g_ref[...]) * (1.0 + mod_ref[0, 1:2, :]) + mod_ref[0, 0:1, :]
    p = _dot(h.astype(BF16), w_ref[...])

    lane = lax.broadcasted_iota(jnp.int32, (tm, LANES), 1)
    lo = lane < HEAD_DIM
    first_half = (lane % HEAD_DIM) < (HEAD_DIM // 2)
    cos = cos_ref[...]
    sin = sin_ref[...]

    def rope(v):
        rot = jnp.where(first_half, pltpu.roll(v, LANES - HEAD_DIM // 2, 1),
                        pltpu.roll(v, HEAD_DIM // 2, 1))
        return v * cos + rot * sin

    def swap(v):
        return pltpu.roll(v, HEAD_DIM, 1)

    def slab(c0):
        return p[:, c0:c0 + LANES]

    for j in range(N_HEADS // 2):
        v = rope(slab(C_Q + j * LANES))
        q_ref[0, 2 * j] = jnp.where(lo, v, 0.0).astype(BF16)
        q_ref[0, 2 * j + 1] = jnp.where(lo, swap(v), 0.0).astype(BF16)

    kc_ref[0] = rope(slab(C_KC))
    vc_ref[0] = slab(C_VC)

    pos = pl.program_id(1) * tm + lax.broadcasted_iota(jnp.int32, (tm, LANES), 0)
    onehot = jnp.where(lane - HEAD_DIM == pos // SEL_BLOCK, 1.0, 0.0)
    v = rope(slab(C_KS))
    ks_ref[0, 0] = jnp.where(lo, v, onehot).astype(BF16)
    ks_ref[0, 1] = jnp.where(lo, swap(v), onehot).astype(BF16)
    v = rope(slab(C_KW))
    kw_ref[0, 0] = jnp.where(lo, v, 0.0).astype(BF16)
    kw_ref[0, 1] = jnp.where(lo, swap(v), 0.0).astype(BF16)

    ones_col = jnp.where(lane == ROWSUM_LANE, 1.0, 0.0)
    for c0, ref in ((C_VS, vs_ref), (C_VW, vw_ref)):
        v = slab(c0)
        ref[0, 0] = jnp.where(lo, v, ones_col).astype(BF16)
        ref[0, 1] = jnp.where(lo, swap(v), ones_col).astype(BF16)

    gates = slab(C_GATE)
    gl_ref[0, 0] = gates
    gl_ref[0, 1] = swap(gates)
    u_ref[0] = p[:, C_U:C_U + POOL_WIDTH]


def _inproj(x, mod, g_pre, w_cat, cos_t, sin_t):
    b, s, d = x.shape
    tm = min(TM_PROJ, s)
    nt = s // tm
    const = lambda bi, i: (0, 0)
    out_shape = (
        jax.ShapeDtypeStruct((b, N_HEADS, s, LANES), BF16),
        jax.ShapeDtypeStruct((b, s, KV_WIDTH), F32),
        jax.ShapeDtypeStruct((b, s, KV_WIDTH), F32),
        jax.ShapeDtypeStruct((b, N_KV, s, LANES), BF16),
        jax.ShapeDtypeStruct((b, N_KV, s, LANES), BF16),
        jax.ShapeDtypeStruct((b, N_KV, s, LANES), BF16),
        jax.ShapeDtypeStruct((b, N_KV, s, LANES), BF16),
        jax.ShapeDtypeStruct((b, N_KV, s, LANES), F32),
        jax.ShapeDtypeStruct((b, s, POOL_WIDTH), F32),
    )
    kv4 = lambda w: pl.BlockSpec((1, N_KV, tm, w), lambda bi, i: (bi, 0, i, 0))
    tok = lambda w: pl.BlockSpec((1, tm, w), lambda bi, i: (bi, i, 0))
    out_specs = (
        pl.BlockSpec((1, N_HEADS, tm, LANES), lambda bi, i: (bi, 0, i, 0)),
        tok(KV_WIDTH), tok(KV_WIDTH), kv4(LANES), kv4(LANES), kv4(LANES), kv4(LANES), kv4(LANES),
        tok(POOL_WIDTH),
    )
    return pl.pallas_call(
        _inproj_kernel,
        grid=(b, nt),
        in_specs=[pl.BlockSpec((1, tm, d), lambda bi, i: (bi, i, 0)),
                  pl.BlockSpec((1, N_MOD, d), lambda bi, i: (bi, 0, 0)),
                  pl.BlockSpec((1, d), const),
                  pl.BlockSpec((d, IN_COLS), const),
                  pl.BlockSpec((tm, LANES), lambda bi, i: (i, 0)),
                  pl.BlockSpec((tm, LANES), lambda bi, i: (i, 0))],
        out_specs=out_specs,
        out_shape=out_shape,
        compiler_params=_cparams(2),
        name="inproj",
    )(x, mod, g_pre, w_cat, cos_t, sin_t)


def _gelu_tanh(x):
    return 0.5 * x * (1.0 + jnp.tanh(np.sqrt(2.0 / np.pi) * (x + 0.044715 * (x * x * x))))


def _compress_kernel(k_ref, v_ref, pek_ref, pev_ref, w1k_ref, w1v_ref, w2k_ref, w2v_ref,
                     kc_ref, vc_ref):
    nc = kc_ref.shape[2]
    lane = lax.broadcasted_iota(jnp.int32, (nc, LANES), 1)
    lo = lane < HEAD_DIM

    def run(x_ref, pe_ref, w1_ref, w2_ref):
        chunks = jnp.concatenate(
            [x_ref[0, pl.ds(l, nc, stride=CMP_STRIDE), :] for l in range(CMP_STRIDE)], axis=1)
        pa = _dot((chunks + pe_ref[0:1, :]).astype(BF16), w1_ref[0])
        pb = _dot((chunks + pe_ref[1:2, :]).astype(BF16), w1_ref[1])
        pb = jnp.concatenate([pb[1:], jnp.zeros((1, LANES), F32)], axis=0)
        hid = _gelu_tanh(pa + pb)
        out = _dot(hid.astype(BF16), w2_ref[...])
        return (jnp.where(lo, out, 0.0).astype(BF16),
                jnp.where(lo, pltpu.roll(out, HEAD_DIM, 1), 0.0).astype(BF16))

    kc_ref[0, 0], kc_ref[0, 1] = run(k_ref, pek_ref, w1k_ref, w2k_ref)
    vc_ref[0, 0], vc_ref[0, 1] = run(v_ref, pev_ref, w1v_ref, w2v_ref)


def _compress_weights(pe, w1, w2):
    eye = jnp.eye(N_KV, dtype=F32)
    pe2 = jnp.tile(pe.reshape(2, CMP_STRIDE, 1, HEAD_DIM), (1, 1, N_KV, 1))
    pe2 = pe2.reshape(2, CMP_STRIDE * KV_WIDTH)
    w1r = w1.reshape(2, CMP_STRIDE, HEAD_DIM, HEAD_DIM)
    w1b = jnp.einsum("hlde,gk->hlgdke", w1r, eye).reshape(2, CMP_STRIDE * KV_WIDTH, KV_WIDTH)
    w2b = jnp.einsum("de,gk->gdke", w2, eye).reshape(KV_WIDTH, KV_WIDTH)
    return pe2, w1b.astype(BF16), w2b.astype(BF16)


def _compress(kc, vc, pek, pev, w1k, w1v, w2k, w2v):
    b, s, width = kc.shape
    nc = s // CMP_STRIDE
    wide = CMP_STRIDE * width
    blk = pl.BlockSpec((1, s, width), lambda bi: (bi, 0, 0))
    c2 = lambda bi: (0, 0)
    c3 = lambda bi: (0, 0, 0)
    out = pl.BlockSpec((1, N_KV, nc, LANES), lambda bi: (bi, 0, 0, 0))
    return pl.pallas_call(
        _compress_kernel,
        grid=(b,),
        in_specs=[blk, blk,
                  pl.BlockSpec((2, wide), c2), pl.BlockSpec((2, wide), c2),
                  pl.BlockSpec((2, wide, width), c3), pl.BlockSpec((2, wide, width), c3),
                  pl.BlockSpec((width, width), c2), pl.BlockSpec((width, width), c2)],
        out_specs=(out, out),
        out_shape=(jax.ShapeDtypeStruct((b, N_KV, nc, LANES), BF16),
                   jax.ShapeDtypeStruct((b, N_KV, nc, LANES), BF16)),
        compiler_params=_cparams(1),
        name="compress",
    )(kc, vc, pek, pev, w1k, w1v, w2k, w2v)


def _attn_kernel(q_ref, kc_ref, vc_ref, ks_ref, vs_ref, kw_ref, vw_ref, gl_ref, ovl_ref, o_ref,
                 val_ref, rank_ref, m_ref, acc_ref):
    tq = q_ref.shape[2]
    nc = kc_ref.shape[2]
    s_len = ks_ref.shape[2]
    rows = GQA_R * tq
    q0 = pl.program_id(2) * tq

    q4 = q_ref[0]
    t_col = q0 + lax.broadcasted_iota(jnp.int32, (tq, 1), 0)
    t_rows = jnp.concatenate([t_col] * GQA_R, axis=0)

    def head(a, r):
        return a[r * tq:(r + 1) * tq]

    def pv_heads(pb, v):
        return [_dot(head(pb, r), v) for r in range(GQA_R)]

    def weights(s, m):
        return jnp.exp2((s - m).astype(BF16))

    q_plain = q4.reshape(rows, LANES)
    sc = _dot_nt(q_plain, kc_ref[0, 0])
    cmp_end = lax.broadcasted_iota(jnp.int32, (1, nc), 1) * CMP_STRIDE + (CMP_BLOCK - 1)
    sc = jnp.where(cmp_end <= t_rows, sc, -jnp.inf)
    m = jnp.max(sc, axis=-1, keepdims=True)
    m = jnp.where(m == -jnp.inf, 0.0, m)
    e = jnp.exp2(sc - m)
    p_cmp = e / jnp.maximum(jnp.sum(e, axis=-1, keepdims=True), 1e-30)
    o_cmp = pv_heads(p_cmp.astype(BF16), vc_ref[0, 0])

    p_sum = p_cmp[0:tq] + p_cmp[tq:2 * tq] + p_cmp[2 * tq:3 * tq] + p_cmp[3 * tq:]
    p_hi = p_sum.astype(BF16)
    rem = p_sum - p_hi.astype(F32)
    p_mid = rem.astype(BF16)
    p_lo = (rem - p_mid.astype(F32)).astype(BF16)
    ovl_t = ovl_ref[...]
    imp = _dot_nt(ovl_t, p_hi) + _dot_nt(ovl_t, p_mid) + _dot_nt(ovl_t, p_lo)
    blk = lax.broadcasted_iota(jnp.int32, (N_BLK, tq), 0)
    cur = (q0 + lax.broadcasted_iota(jnp.int32, (1, tq), 1)) // SEL_BLOCK
    causal = blk <= cur
    forced = (blk == 0) | (causal & (blk > cur - N_FORCED_LOCAL))
    val = jnp.where(forced, FORCE_SCORE, jnp.where(causal, imp, NOT_CAUSAL_SCORE))

    n_tiles = N_BLK // SUBLANES
    row8 = lax.broadcasted_iota(jnp.int32, (SUBLANES, tq), 0)
    val_ref[...] = val
    rank_ref[...] = jnp.zeros((N_BLK, tq), F32)
    last_causal = (q0 + tq - 1) // SEL_BLOCK
    for src in range(n_tiles):
        @pl.when(src * SUBLANES <= last_causal)
        def _(src=src):
            rows_b = [jnp.broadcast_to(val_ref[mp:mp + 1, :], (SUBLANES, tq))
                      for mp in range(src * SUBLANES, (src + 1) * SUBLANES)]
            for dst in range(n_tiles):
                sl = slice(dst * SUBLANES, (dst + 1) * SUBLANES)
                v = val_ref[sl, :]
                acc = jnp.zeros((SUBLANES, tq), F32)
                for i, b in enumerate(rows_b):
                    if src < dst:
                        inc = jnp.where(b >= v, 1.0, 0.0)
                    elif src > dst:
                        inc = jnp.where(b > v, 1.0, 0.0)
                    else:
                        inc = jnp.where(row8 > i, jnp.where(b >= v, 1.0, 0.0),
                                        jnp.where(b > v, 1.0, 0.0))
                    acc = acc + inc
                rank_ref[sl, :] += acc
    rank = rank_ref[...]
    bias_t = jnp.where(rank < float(SEL_TOPK), 0.0, BLOCK_MASK_BIAS)
    bias = jnp.concatenate([jnp.zeros((HEAD_DIM, tq), F32), bias_t], axis=0).T
    q_aug = (q4 + bias.astype(BF16)[None]).reshape(rows, LANES)

    def sel_step(j, _, base, tk, diagonal):
        k0 = pl.multiple_of(base + j * tk, tk)
        s = _dot_nt(q_aug, ks_ref[0, 0, pl.ds(k0, tk), :])
        if diagonal:
            kpos = k0 + lax.broadcasted_iota(jnp.int32, (1, tk), 1)
            s = jnp.where(kpos <= t_rows, s, -jnp.inf)
        m_i = m_ref[...]
        m_new = jnp.maximum(m_i, jnp.max(s, axis=-1, keepdims=True))
        m_ref[...] = m_new
        alpha = jnp.exp2(m_i - m_new)
        pb = jnp.exp2((s - pltpu.repeat(m_new, tk // LANES, axis=1)).astype(BF16))
        outs = pv_heads(pb, vs_ref[0, 0, pl.ds(k0, tk), :])
        for r in range(GQA_R):
            acc_ref[r] = head(alpha, r) * acc_ref[r] + outs[r]
        return 0

    tk_diag = min(SEL_TILES[-1], s_len)
    diag0 = (q0 // tk_diag) * tk_diag
    m_ref[...] = jnp.full((rows, LANES), -jnp.inf, F32)
    acc_ref[...] = jnp.zeros((GQA_R, tq, LANES), F32)
    base = 0
    for tk in SEL_TILES:
        tk = min(tk, s_len)
        n_tk = (diag0 - base) // tk
        lax.fori_loop(0, n_tk, functools.partial(sel_step, base=base, tk=tk, diagonal=False), 0)
        base = base + n_tk * tk
    sel_step(0, 0, base=diag0, tk=tk_diag, diagonal=True)
    o_sel = [acc_ref[r] for r in range(GQA_R)]

    wlen = min(WINDOW + tq, s_len)
    w0 = pl.multiple_of(jnp.maximum(q0 + tq - wlen, 0), tq)
    sw = _dot_nt(q_plain, kw_ref[0, 0, pl.ds(w0, wlen), :])
    kp = w0 + lax.broadcasted_iota(jnp.int32, (1, wlen), 1)
    age = lax.bitcast_convert_type(t_rows - kp[:, :tq], jnp.uint32)
    sw = jnp.concatenate([jnp.where(age < WINDOW, sw[:, :tq], -jnp.inf),
                          jnp.where(kp[:, tq:] <= t_rows, sw[:, tq:], -jnp.inf)], axis=1)
    o_win = pv_heads(weights(sw, jnp.max(sw, axis=-1, keepdims=True)),
                     vw_ref[0, 0, pl.ds(w0, wlen), :])

    sg = jax.nn.sigmoid(gl_ref[0, 0])
    rowsum = lambda a: a[:, ROWSUM_LANE:ROWSUM_LANE + 1]
    heads = []
    for r in range(GQA_R):
        g_cmp, g_sel, g_win = (sg[:, 3 * r + j:3 * r + j + 1] for j in range(3))
        o = (g_cmp * o_cmp[r] + (g_sel / rowsum(o_sel[r])) * o_sel[r]
             + (g_win / rowsum(o_win[r])) * o_win[r])
        heads.append(o[:, :HEAD_DIM])
    o_ref[0] = jnp.concatenate(heads, axis=1).astype(BF16)


def _attention(q, kc_aug, vc_p, ks_aug, vs_p, kw_aug, vw_p, gl, ovl):
    b, _, s, _ = q.shape
    nc = kc_aug.shape[2]
    tq = min(TQ, s)
    kv = lambda n: pl.BlockSpec((1, 1, n, LANES), lambda bi, gi, i: (bi, gi, 0, 0))
    return pl.pallas_call(
        _attn_kernel,
        grid=(b, N_KV, s // tq),
        in_specs=[pl.BlockSpec((1, GQA_R, tq, LANES), lambda bi, gi, i: (bi, gi, i, 0)),
                  kv(nc), kv(nc), kv(s), kv(s), kv(s), kv(s),
                  pl.BlockSpec((1, 1, tq, LANES), lambda bi, gi, i: (bi, gi, i, 0)),
                  pl.BlockSpec((N_BLK, nc), lambda bi, gi, i: (0, 0))],
        out_specs=pl.BlockSpec((1, tq, GQA_R * HEAD_DIM), lambda bi, gi, i: (bi, i, gi)),
        out_shape=jax.ShapeDtypeStruct((b, s, NSA_WIDTH), BF16),
        scratch_shapes=[pltpu.VMEM((N_BLK, tq), F32), pltpu.VMEM((N_BLK, tq), F32),
                        pltpu.VMEM((GQA_R * tq, LANES), F32), pltpu.VMEM((GQA_R, tq, LANES), F32)],
        compiler_params=_cparams(3),
        name="nsa_attention",
    )(q, kc_aug, vc_p, ks_aug, vs_p, kw_aug, vw_p, gl, ovl)


def _merge_kernel(x_ref, mod_ref, gpre_ref, gpost_ref, on_ref, u_ref, uprev_ref, wbrg_ref,
                  wpool_ref, pscale_ref, wbn_ref, wbp_ref, wout_ref, o_ref):
    tm = x_ref.shape[1]
    i = pl.program_id(1)
    x = x_ref[0]
    h = (_rms(x) * gpre_ref[...]) * (1.0 + mod_ref[0, 1:2, :]) + mod_ref[0, 0:1, :]
    br = jax.nn.sigmoid(_dot(h.astype(BF16), wbrg_ref[...]))

    u = u_ref[0]
    prev = jnp.where(i > 0, uprev_ref[0], 0.0)
    ext = jnp.concatenate([prev, u], axis=0)
    tpos = i * tm + lax.broadcasted_iota(jnp.int32, (tm, 1), 0)
    mixed = []
    for gidx, w in enumerate(POOL_SIZES):
        acc = ext[:, gidx * POOL_GROUP:(gidx + 1) * POOL_GROUP]
        span = 1
        while span < w:
            acc = acc + pltpu.roll(acc, span, 0)
            span *= 2
        cnt = jnp.minimum(tpos + 1, w).astype(F32)
        pooled = acc[POOL_HALO:] / cnt - u[:, gidx * POOL_GROUP:(gidx + 1) * POOL_GROUP]
        mixed.append(_dot(pooled.astype(BF16), wpool_ref[gidx]))
    o_pool = jnp.concatenate(mixed, axis=1) * pscale_ref[...]

    merged = (br[:, :D_MODEL] * _dot(on_ref[0], wbn_ref[...])
              + br[:, D_MODEL:] * _dot(o_pool.astype(BF16), wbp_ref[...]))
    y = _dot(merged.astype(BF16), wout_ref[...])
    o_ref[0] = x + mod_ref[0, 2:3, :] * (_rms(y) * gpost_ref[...])


def _merge(x, mod, g_pre, g_post, o_nsa, u, w_brg, w_pool, pool_scale, w_br_nsa, w_br_pool, w_out):
    b, s, d = x.shape
    tm = min(TM_PROJ, s)
    halo_per_tile = tm // POOL_HALO
    tile = lambda w: pl.BlockSpec((1, tm, w), lambda bi, i: (bi, i, 0))
    c2 = lambda shape: pl.BlockSpec(shape, lambda bi, i: (0, 0))
    return pl.pallas_call(
        _merge_kernel,
        grid=(b, s // tm),
        in_specs=[tile(d),
                  pl.BlockSpec((1, N_MOD, d), lambda bi, i: (bi, 0, 0)),
                  c2((1, d)), c2((1, d)),
                  tile(NSA_WIDTH), tile(POOL_WIDTH),
                  pl.BlockSpec((1, POOL_HALO, POOL_WIDTH),
                               lambda bi, i: (bi, jnp.maximum(i * halo_per_tile - 1, 0), 0)),
                  c2((d, 2 * d)),
                  pl.BlockSpec((len(POOL_SIZES), POOL_GROUP, POOL_GROUP), lambda bi, i: (0, 0, 0)),
                  c2((1, POOL_WIDTH)),
                  c2((NSA_WIDTH, d)), c2((POOL_WIDTH, d)), c2((d, d))],
        out_specs=tile(d),
        out_shape=jax.ShapeDtypeStruct((b, s, d), F32),
        compiler_params=_cparams(2),
        name="merge",
    )(x, mod, g_pre, g_post, o_nsa, u, u, w_brg, w_pool, pool_scale, w_br_nsa, w_br_pool, w_out)


def _mlp_kernel(x_ref, mod_ref, gpre_ref, gpost_ref, w1_ref, w2_ref, o_ref):
    x = x_ref[0]
    h = ((_rms(x) * gpre_ref[...]) * (1.0 + mod_ref[0, 4:5, :]) + mod_ref[0, 3:4, :]).astype(BF16)
    y = jnp.zeros(x.shape, F32)
    for c0 in range(0, D_FF, FF_CHUNK):
        a = jnp.maximum(_dot(h, w1_ref[:, c0:c0 + FF_CHUNK]), 0.0)
        y = y + _dot((a * a).astype(BF16), w2_ref[c0:c0 + FF_CHUNK, :])
    o_ref[0] = x + mod_ref[0, 5:6, :] * (_rms(y) * gpost_ref[...])


def _mlp(x, mod, g_pre, g_post, w_fc1, w_fc2):
    b, s, d = x.shape
    tm = min(TM_PROJ, s)
    tile = pl.BlockSpec((1, tm, d), lambda bi, i: (bi, i, 0))
    resident = lambda shape: pl.BlockSpec(shape, lambda bi, i: (0, 0), pipeline_mode=pl.Buffered(1))
    return pl.pallas_call(
        _mlp_kernel,
        grid=(b, s // tm),
        in_specs=[tile,
                  pl.BlockSpec((1, N_MOD, d), lambda bi, i: (bi, 0, 0)),
                  pl.BlockSpec((1, d), lambda bi, i: (0, 0)),
                  pl.BlockSpec((1, d), lambda bi, i: (0, 0)),
                  resident((d, D_FF)), resident((D_FF, d))],
        out_specs=tile,
        out_shape=jax.ShapeDtypeStruct((b, s, d), F32),
        compiler_params=_cparams(2),
        name="mlp",
    )(x, mod, g_pre, g_post, w_fc1, w_fc2)


def _rope_tables(s):
    half = HEAD_DIM // 2
    inv = ROPE_THETA ** (-jnp.arange(half, dtype=F32) / half)
    ang = jnp.arange(s, dtype=F32)[:, None] * inv[None, :]
    cos = jnp.cos(ang)
    sin = jnp.sin(ang)
    reps = LANES // HEAD_DIM
    cos_t = jnp.tile(jnp.concatenate([cos, cos], axis=1), (1, reps))
    sin_t = jnp.tile(jnp.concatenate([-sin, sin], axis=1), (1, reps))
    return cos_t, sin_t


def _overlap_table(nc, n_sel):
    cmp_start = np.arange(nc) * CMP_STRIDE
    sel_start = np.arange(n_sel) * SEL_BLOCK
    ovl = ((cmp_start[None, :] < sel_start[:, None] + SEL_BLOCK)
           & (cmp_start[None, :] + CMP_BLOCK > sel_start[:, None])).astype(np.float32)
    out = np.zeros((N_BLK, nc), np.float32)
    out[:n_sel] = ovl
    return jnp.asarray(out, dtype=BF16)


def _pack_in_weights(w_in):
    col_scale = jnp.concatenate([jnp.full((NSA_WIDTH,), HEAD_DIM ** -0.5 * LOG2E, F32),
                                 jnp.ones((w_in.shape[1] - NSA_WIDTH,), F32)])
    w = (w_in * col_scale).astype(BF16)
    q, kc, vc, ks, vs, kw, vw, gate, u, brg = jnp.split(w, IN_SPLITS, axis=-1)
    d = w_in.shape[0]
    gate = gate.reshape(d, N_KV, GQA_R * 3)
    gate = jnp.pad(gate, ((0, 0), (0, 0), (0, HEAD_DIM - GQA_R * 3))).reshape(d, KV_WIDTH)
    return jnp.concatenate([q, kc, ks, kw, vc, vs, vw, gate, u], axis=1), brg


def kernel(x, c, w_ada, b_ada, g_pre_mix, g_post_mix, w_in, cmp_pe_k, cmp_w1_k, cmp_w2_k,
           cmp_pe_v, cmp_w1_v, cmp_w2_v, w_pool, pool_scale, w_br_nsa, w_br_pool, w_out,
           g_pre_mlp, g_post_mlp, w_fc1, w_fc2):
    b, s, d = x.shape
    depth = w_ada.shape[0]
    n_sel = s // SEL_BLOCK
    nc = s // CMP_STRIDE
    assert d == D_MODEL and s % TM_PROJ == 0 and s >= WINDOW + TQ
    assert SEL_TOPK <= n_sel <= N_BLK
    cos_t, sin_t = _rope_tables(s)
    ovl = _overlap_table(nc, n_sel)

    for l in range(depth):
        mod = _modulation(c, w_ada, b_ada, l).reshape(b, N_MOD, d)
        w_cat, w_brg = _pack_in_weights(w_in[l])
        q, kc, vc, ks_aug, kw_aug, vs_p, vw_p, gl, u = _inproj(
            x, mod, g_pre_mix[l].reshape(1, d), w_cat, cos_t, sin_t)
        pek, w1k, w2k = _compress_weights(cmp_pe_k[l], cmp_w1_k[l], cmp_w2_k[l])
        pev, w1v, w2v = _compress_weights(cmp_pe_v[l], cmp_w1_v[l], cmp_w2_v[l])
        kc_aug, vc_p = _compress(kc, vc, pek, pev, w1k, w1v, w2k, w2v)
        o_nsa = _attention(q, kc_aug, vc_p, ks_aug, vs_p, kw_aug, vw_p, gl, ovl)
        x = _merge(x, mod, g_pre_mix[l].reshape(1, d), g_post_mix[l].reshape(1, d), o_nsa, u,
                   w_brg, w_pool[l].astype(BF16), pool_scale[l].reshape(1, POOL_WIDTH),
                   w_br_nsa[l].astype(BF16), w_br_pool[l].astype(BF16), w_out[l].astype(BF16))
        x = _mlp(x, mod, g_pre_mlp[l].reshape(1, d), g_post_mlp[l].reshape(1, d),
                 w_fc1[l].astype(BF16), w_fc2[l].astype(BF16))
    return x
```

```python
import functools

import numpy as np
import jax
import jax.numpy as jnp
from jax import lax
from jax.experimental import pallas as pl
from jax.experimental.pallas import tpu as pltpu

F32 = jnp.float32
BF16 = jnp.bfloat16

D_MODEL = 1024
N_HEADS = 8
N_KV = 2
HEAD_DIM = 64
GQA_R = N_HEADS // N_KV
NSA_WIDTH = N_HEADS * HEAD_DIM
KV_WIDTH = N_KV * HEAD_DIM
CMP_BLOCK = 32
CMP_STRIDE = 16
SEL_BLOCK = 64
SEL_TOPK = 16
N_FORCED_LOCAL = 2
WINDOW = 512
FORCE_SCORE = 1e9
POOL_SIZES = (2, 4, 8, 16)
POOL_GROUP = 128
POOL_WIDTH = POOL_GROUP * len(POOL_SIZES)
D_FF = 4 * D_MODEL
ROPE_THETA = 10000.0
EPS = 1e-6
N_MOD = 6
IN_SIZES = (NSA_WIDTH, KV_WIDTH, KV_WIDTH, KV_WIDTH, KV_WIDTH, KV_WIDTH, KV_WIDTH,
            3 * N_HEADS, POOL_WIDTH, 2 * D_MODEL)
IN_SPLITS = [int(v) for v in np.cumsum(IN_SIZES)[:-1]]

LANES = 128
SUBLANES = 8
N_BLK = LANES - HEAD_DIM
ROWSUM_LANE = HEAD_DIM
LOG2E = 1.4426950408889634
POOL_HALO = 16
BLOCK_MASK_BIAS = -float(2 ** 30)
NOT_CAUSAL_SCORE = -1e30

C_Q = 0
C_KC = C_Q + NSA_WIDTH
C_KS = C_KC + KV_WIDTH
C_KW = C_KS + KV_WIDTH
C_VC = C_KW + KV_WIDTH
C_VS = C_VC + KV_WIDTH
C_VW = C_VS + KV_WIDTH
C_GATE = C_VW + KV_WIDTH
C_U = C_GATE + LANES
IN_COLS = C_U + POOL_WIDTH

TM_PROJ = 512
TQ = 256
SEL_TILES = (2048, 1024, 512)
FF_CHUNK = 1024
VMEM_LIMIT = 56 * 1024 * 1024


def _cparams(n_axes):
    return pltpu.CompilerParams(dimension_semantics=("arbitrary",) * n_axes,
                                vmem_limit_bytes=VMEM_LIMIT)


def _dot(a, b):
    return jnp.dot(a, b, preferred_element_type=F32)


def _dot_nt(a, b):
    return lax.dot_general(a, b, (((1,), (1,)), ((), ())), preferred_element_type=F32)


def _rms(x):
    return x * lax.rsqrt(jnp.mean(x * x, axis=-1, keepdims=True) + EPS)


def _mod_kernel(c_ref, w_ref, b_ref, o_ref):
    c = c_ref[...]
    a = c * jax.nn.sigmoid(c)
    o_ref[...] = jnp.dot(a, w_ref[...], preferred_element_type=F32,
                         precision=lax.Precision.HIGHEST) + b_ref[...]


def _modulation(c, w_ada, b_ada, layer):
    b, d = c.shape
    n = w_ada.shape[2]
    tn = 2048
    return pl.pallas_call(
        _mod_kernel,
        grid=(n // tn,),
        in_specs=[pl.BlockSpec((b, d), lambda j: (0, 0)),
                  pl.BlockSpec((None, d, tn), lambda j: (layer, 0, j)),
                  pl.BlockSpec((None, 1, tn), lambda j: (layer, 0, j))],
        out_specs=pl.BlockSpec((b, tn), lambda j: (0, j)),
        out_shape=jax.ShapeDtypeStruct((b, n), F32),
        compiler_params=_cparams(1),
        name="modulation",
    )(c, w_ada, b_ada.reshape(-1, 1, n))


def _inproj_kernel(x_ref, mod_ref, g_ref, w_ref, cos_ref, sin_ref,
                   q_ref, kc_ref, vc_ref, ks_ref, kw_ref, vs_ref, vw_ref, gl_ref, u_ref):
    tm = x_ref.shape[1]
    h = (_rms(x_ref[0]) * g_ref[...]) * (1.0 + mod_ref[0, 1:2, :]) + mod_ref[0, 0:1, :]
    p = _dot(h.astype(BF16), w_ref[...])

    lane = lax.broadcasted_iota(jnp.int32, (tm, LANES), 1)
    lo = lane < HEAD_DIM
    first_half = (lane % HEAD_DIM) < (HEAD_DIM // 2)
    cos = cos_ref[...]
    sin = sin_ref[...]

    def rope(v):
        rot = jnp.where(first_half, pltpu.roll(v, LANES - HEAD_DIM // 2, 1),
                        pltpu.roll(v, HEAD_DIM // 2, 1))
        return v * cos + rot * sin

    def swap(v):
        return pltpu.roll(v, HEAD_DIM, 1)

    def slab(c0):
        return p[:, c0:c0 + LANES]

    for j in range(N_HEADS // 2):
        v = rope(slab(C_Q + j * LANES))
        q_ref[0, 2 * j] = jnp.where(lo, v, 0.0).astype(BF16)
        q_ref[0, 2 * j + 1] = jnp.where(lo, swap(v), 0.0).astype(BF16)

    kc_ref[0] = rope(slab(C_KC))
    vc_ref[0] = slab(C_VC)

    pos = pl.program_id(1) * tm + lax.broadcasted_iota(jnp.int32, (tm, LANES), 0)
    onehot = jnp.where(lane - HEAD_DIM == pos // SEL_BLOCK, 1.0, 0.0)
    v = rope(slab(C_KS))
    ks_ref[0, 0] = jnp.where(lo, v, onehot).astype(BF16)
    ks_ref[0, 1] = jnp.where(lo, swap(v), onehot).astype(BF16)
    v = rope(slab(C_KW))
    kw_ref[0, 0] = jnp.where(lo, v, 0.0).astype(BF16)
    kw_ref[0, 1] = jnp.where(lo, swap(v), 0.0).astype(BF16)

    ones_col = jnp.where(lane == ROWSUM_LANE, 1.0, 0.0)
    for c0, ref in ((C_VS, vs_ref), (C_VW, vw_ref)):
        v = slab(c0)
        ref[0, 0] = jnp.where(lo, v, ones_col).astype(BF16)
        ref[0, 1] = jnp.where(lo, swap(v), ones_col).astype(BF16)

    gates = slab(C_GATE)
    gl_ref[0, 0] = gates
    gl_ref[0, 1] = swap(gates)
    u_ref[0] = p[:, C_U:C_U + POOL_WIDTH]


def _inproj(x, mod, g_pre, w_cat, cos_t, sin_t):
    b, s, d = x.shape
    tm = min(TM_PROJ, s)
    nt = s // tm
    const = lambda bi, i: (0, 0)
    out_shape = (
        jax.ShapeDtypeStruct((b, N_HEADS, s, LANES), BF16),
        jax.ShapeDtypeStruct((b, s, KV_WIDTH), F32),
        jax.ShapeDtypeStruct((b, s, KV_WIDTH), F32),
        jax.ShapeDtypeStruct((b, N_KV, s, LANES), BF16),
        jax.ShapeDtypeStruct((b, N_KV, s, LANES), BF16),
        jax.ShapeDtypeStruct((b, N_KV, s, LANES), BF16),
        jax.ShapeDtypeStruct((b, N_KV, s, LANES), BF16),
        jax.ShapeDtypeStruct((b, N_KV, s, LANES), F32),
        jax.ShapeDtypeStruct((b, s, POOL_WIDTH), F32),
    )
    kv4 = lambda w: pl.BlockSpec((1, N_KV, tm, w), lambda bi, i: (bi, 0, i, 0))
    tok = lambda w: pl.BlockSpec((1, tm, w), lambda bi, i: (bi, i, 0))
    out_specs = (
        pl.BlockSpec((1, N_HEADS, tm, LANES), lambda bi, i: (bi, 0, i, 0)),
        tok(KV_WIDTH), tok(KV_WIDTH), kv4(LANES), kv4(LANES), kv4(LANES), kv4(LANES), kv4(LANES),
        tok(POOL_WIDTH),
    )
    return pl.pallas_call(
        _inproj_kernel,
        grid=(b, nt),
        in_specs=[pl.BlockSpec((1, tm, d), lambda bi, i: (bi, i, 0)),
                  pl.BlockSpec((1, N_MOD, d), lambda bi, i: (bi, 0, 0)),
                  pl.BlockSpec((1, d), const),
                  pl.BlockSpec((d, IN_COLS), const),
                  pl.BlockSpec((tm, LANES), lambda bi, i: (i, 0)),
                  pl.BlockSpec((tm, LANES), lambda bi, i: (i, 0))],
        out_specs=out_specs,
        out_shape=out_shape,
        compiler_params=_cparams(2),
        name="inproj",
    )(x, mod, g_pre, w_cat, cos_t, sin_t)


def _gelu_tanh(x):
    return 0.5 * x * (1.0 + jnp.tanh(np.sqrt(2.0 / np.pi) * (x + 0.044715 * (x * x * x))))


def _compress_kernel(k_ref, v_ref, pek_ref, pev_ref, w1k_ref, w1v_ref, w2k_ref, w2v_ref,
                     kc_ref, vc_ref):
    nc = kc_ref.shape[2]
    lane = lax.broadcasted_iota(jnp.int32, (nc, LANES), 1)
    lo = lane < HEAD_DIM

    def run(x_ref, pe_ref, w1_ref, w2_ref):
        chunks = jnp.concatenate(
            [x_ref[0, pl.ds(l, nc, stride=CMP_STRIDE), :] for l in range(CMP_STRIDE)], axis=1)
        pa = _dot((chunks + pe_ref[0:1, :]).astype(BF16), w1_ref[0])
        pb = _dot((chunks + pe_ref[1:2, :]).astype(BF16), w1_ref[1])
        pb = jnp.concatenate([pb[1:], jnp.zeros((1, LANES), F32)], axis=0)
        hid = _gelu_tanh(pa + pb)
        out = _dot(hid.astype(BF16), w2_ref[...])
        return (jnp.where(lo, out, 0.0).astype(BF16),
                jnp.where(lo, pltpu.roll(out, HEAD_DIM, 1), 0.0).astype(BF16))

    kc_ref[0, 0], kc_ref[0, 1] = run(k_ref, pek_ref, w1k_ref, w2k_ref)
    vc_ref[0, 0], vc_ref[0, 1] = run(v_ref, pev_ref, w1v_ref, w2v_ref)


def _compress_weights(pe, w1, w2):
    eye = jnp.eye(N_KV, dtype=F32)
    pe2 = jnp.tile(pe.reshape(2, CMP_STRIDE, 1, HEAD_DIM), (1, 1, N_KV, 1))
    pe2 = pe2.reshape(2, CMP_STRIDE * KV_WIDTH)
    w1r = w1.reshape(2, CMP_STRIDE, HEAD_DIM, HEAD_DIM)
    w1b = jnp.einsum("hlde,gk->hlgdke", w1r, eye).reshape(2, CMP_STRIDE * KV_WIDTH, KV_WIDTH)
    w2b = jnp.einsum("de,gk->gdke", w2, eye).reshape(KV_WIDTH, KV_WIDTH)
    return pe2, w1b.astype(BF16), w2b.astype(BF16)


def _compress(kc, vc, pek, pev, w1k, w1v, w2k, w2v):
    b, s, width = kc.shape
    nc = s // CMP_STRIDE
    wide = CMP_STRIDE * width
    blk = pl.BlockSpec((1, s, width), lambda bi: (bi, 0, 0))
    c2 = lambda bi: (0, 0)
    c3 = lambda bi: (0, 0, 0)
    out = pl.BlockSpec((1, N_KV, nc, LANES), lambda bi: (bi, 0, 0, 0))
    return pl.pallas_call(
        _compress_kernel,
        grid=(b,),
        in_specs=[blk, blk,
                  pl.BlockSpec((2, wide), c2), pl.BlockSpec((2, wide), c2),
                  pl.BlockSpec((2, wide, width), c3), pl.BlockSpec((2, wide, width), c3),
                  pl.BlockSpec((width, width), c2), pl.BlockSpec((width, width), c2)],
        out_specs=(out, out),
        out_shape=(jax.ShapeDtypeStruct((b, N_KV, nc, LANES), BF16),
                   jax.ShapeDtypeStruct((b, N_KV, nc, LANES), BF16)),
        compiler_params=_cparams(1),
        name="compress",
    )(kc, vc, pek, pev, w1k, w1v, w2k, w2v)


def _attn_kernel(q_ref, kc_ref, vc_ref, ks_ref, vs_ref, kw_ref, vw_ref, gl_ref, ovl_ref, o_ref,
                 val_ref, rank_ref, m_ref, acc_ref):
    tq = q_ref.shape[2]
    nc = kc_ref.shape[2]
    s_len = ks_ref.shape[2]
    rows = GQA_R * tq
    q0 = pl.program_id(2) * tq

    q4 = q_ref[0]
    t_col = q0 + lax.broadcasted_iota(jnp.int32, (tq, 1), 0)
    t_rows = jnp.concatenate([t_col] * GQA_R, axis=0)

    def head(a, r):
        return a[r * tq:(r + 1) * tq]

    def pv_heads(pb, v):
        return [_dot(head(pb, r), v) for r in range(GQA_R)]

    def weights(s, m):
        return jnp.exp2((s - m).astype(BF16))

    q_plain = q4.reshape(rows, LANES)
    sc = _dot_nt(q_plain, kc_ref[0, 0])
    cmp_end = lax.broadcasted_iota(jnp.int32, (1, nc), 1) * CMP_STRIDE + (CMP_BLOCK - 1)
    sc = jnp.where(cmp_end <= t_rows, sc, -jnp.inf)
    m = jnp.max(sc, axis=-1, keepdims=True)
    m = jnp.where(m == -jnp.inf, 0.0, m)
    e = jnp.exp2(sc - m)
    p_cmp = e / jnp.maximum(jnp.sum(e, axis=-1, keepdims=True), 1e-30)
    o_cmp = pv_heads(p_cmp.astype(BF16), vc_ref[0, 0])

    p_sum = p_cmp[0:tq] + p_cmp[tq:2 * tq] + p_cmp[2 * tq:3 * tq] + p_cmp[3 * tq:]
    p_hi = p_sum.astype(BF16)
    rem = p_sum - p_hi.astype(F32)
    p_mid = rem.astype(BF16)
    p_lo = (rem - p_mid.astype(F32)).astype(BF16)
    ovl_t = ovl_ref[...]
    imp = _dot_nt(ovl_t, p_hi) + _dot_nt(ovl_t, p_mid) + _dot_nt(ovl_t, p_lo)
    blk = lax.broadcasted_iota(jnp.int32, (N_BLK, tq), 0)
    cur = (q0 + lax.broadcasted_iota(jnp.int32, (1, tq), 1)) // SEL_BLOCK
    causal = blk <= cur
    forced = (blk == 0) | (causal & (blk > cur - N_FORCED_LOCAL))
    val = jnp.where(forced, FORCE_SCORE, jnp.where(causal, imp, NOT_CAUSAL_SCORE))

    n_tiles = N_BLK // SUBLANES
    row8 = lax.broadcasted_iota(jnp.int32, (SUBLANES, tq), 0)
    val_ref[...] = val
    rank_ref[...] = jnp.zeros((N_BLK, tq), F32)
    last_causal = (q0 + tq - 1) // SEL_BLOCK
    for src in range(n_tiles):
        @pl.when(src * SUBLANES <= last_causal)
        def _(src=src):
            rows_b = [jnp.broadcast_to(val_ref[mp:mp + 1, :], (SUBLANES, tq))
                      for mp in range(src * SUBLANES, (src + 1) * SUBLANES)]
            for dst in range(n_tiles):
                sl = slice(dst * SUBLANES, (dst + 1) * SUBLANES)
                v = val_ref[sl, :]
                acc = jnp.zeros((SUBLANES, tq), F32)
                for i, b in enumerate(rows_b):
                    if src < dst:
                        inc = jnp.where(b >= v, 1.0, 0.0)
                    elif src > dst:
                        inc = jnp.where(b > v, 1.0, 0.0)
                    else:
                        inc = jnp.where(row8 > i, jnp.where(b >= v, 1.0, 0.0),
                                        jnp.where(b > v, 1.0, 0.0))
                    acc = acc + inc
                rank_ref[sl, :] += acc
    rank = rank_ref[...]
    bias_t = jnp.where(rank < float(SEL_TOPK), 0.0, BLOCK_MASK_BIAS)
    bias = jnp.concatenate([jnp.zeros((HEAD_DIM, tq), F32), bias_t], axis=0).T
    q_aug = (q4 + bias.astype(BF16)[None]).reshape(rows, LANES)

    def sel_step(j, _, base, tk, diagonal):
        k0 = pl.multiple_of(base + j * tk, tk)
        s = _dot_nt(q_aug, ks_ref[0, 0, pl.ds(k0, tk), :])
        if diagonal:
            kpos = k0 + lax.broadcasted_iota(jnp.int32, (1, tk), 1)
            s = jnp.where(kpos <= t_rows, s, -jnp.inf)
        m_i = m_ref[...]
        m_new = jnp.maximum(m_i, jnp.max(s, axis=-1, keepdims=True))
        m_ref[...] = m_new
        alpha = jnp.exp2(m_i - m_new)
        m_wide = jnp.concatenate([m_new] * (tk // LANES), axis=1)
        pb = jnp.exp2((s - m_wide).astype(BF16))
        outs = pv_heads(pb, vs_ref[0, 0, pl.ds(k0, tk), :])
        for r in range(GQA_R):
            acc_ref[r] = head(alpha, r) * acc_ref[r] + outs[r]
        return 0

    tk_diag = min(SEL_TILES[-1], s_len)
    diag0 = (q0 // tk_diag) * tk_diag
    m_ref[...] = jnp.full((rows, LANES), -jnp.inf, F32)
    acc_ref[...] = jnp.zeros((GQA_R, tq, LANES), F32)
    base = 0
    for tk in SEL_TILES:
        tk = min(tk, s_len)
        n_tk = (diag0 - base) // tk
        lax.fori_loop(0, n_tk, functools.partial(sel_step, base=base, tk=tk, diagonal=False), 0)
        base = base + n_tk * tk
    sel_step(0, 0, base=diag0, tk=tk_diag, diagonal=True)
    o_sel = [acc_ref[r] for r in range(GQA_R)]

    wlen = min(WINDOW + tq, s_len)
    w0 = pl.multiple_of(jnp.maximum(q0 + tq - wlen, 0), tq)
    sw = _dot_nt(q_plain, kw_ref[0, 0, pl.ds(w0, wlen), :])
    kp = w0 + lax.broadcasted_iota(jnp.int32, (1, wlen), 1)
    age = lax.bitcast_convert_type(t_rows - kp[:, :tq], jnp.uint32)
    sw = jnp.concatenate([jnp.where(age < WINDOW, sw[:, :tq], -jnp.inf),
                          jnp.where(kp[:, tq:] <= t_rows, sw[:, tq:], -jnp.inf)], axis=1)
    o_win = pv_heads(weights(sw, jnp.max(sw, axis=-1, keepdims=True)),
                     vw_ref[0, 0, pl.ds(w0, wlen), :])

    sg = jax.nn.sigmoid(gl_ref[0, 0])
    rowsum = lambda a: a[:, ROWSUM_LANE:ROWSUM_LANE + 1]
    heads = []
    for r in range(GQA_R):
        g_cmp, g_sel, g_win = (sg[:, 3 * r + j:3 * r + j + 1] for j in range(3))
        o = (g_cmp * o_cmp[r] + (g_sel / rowsum(o_sel[r])) * o_sel[r]
             + (g_win / rowsum(o_win[r])) * o_win[r])
        heads.append(o[:, :HEAD_DIM])
    o_ref[0] = jnp.concatenate(heads, axis=1).astype(BF16)


def _attention(q, kc_aug, vc_p, ks_aug, vs_p, kw_aug, vw_p, gl, ovl):
    b, _, s, _ = q.shape
    nc = kc_aug.shape[2]
    tq = min(TQ, s)
    kv = lambda n: pl.BlockSpec((1, 1, n, LANES), lambda bi, gi, i: (bi, gi, 0, 0))
    return pl.pallas_call(
        _attn_kernel,
        grid=(b, N_KV, s // tq),
        in_specs=[pl.BlockSpec((1, GQA_R, tq, LANES), lambda bi, gi, i: (bi, gi, i, 0)),
                  kv(nc), kv(nc), kv(s), kv(s), kv(s), kv(s),
                  pl.BlockSpec((1, 1, tq, LANES), lambda bi, gi, i: (bi, gi, i, 0)),
                  pl.BlockSpec((N_BLK, nc), lambda bi, gi, i: (0, 0))],
        out_specs=pl.BlockSpec((1, tq, GQA_R * HEAD_DIM), lambda bi, gi, i: (bi, i, gi)),
        out_shape=jax.ShapeDtypeStruct((b, s, NSA_WIDTH), BF16),
        scratch_shapes=[pltpu.VMEM((N_BLK, tq), F32), pltpu.VMEM((N_BLK, tq), F32),
                        pltpu.VMEM((GQA_R * tq, LANES), F32), pltpu.VMEM((GQA_R, tq, LANES), F32)],
        compiler_params=_cparams(3),
        name="nsa_attention",
    )(q, kc_aug, vc_p, ks_aug, vs_p, kw_aug, vw_p, gl, ovl)


def _merge_kernel(x_ref, mod_ref, gpre_ref, gpost_ref, on_ref, u_ref, uprev_ref, wbrg_ref,
                  wpool_ref, pscale_ref, wbn_ref, wbp_ref, wout_ref, o_ref):
    tm = x_ref.shape[1]
    i = pl.program_id(1)
    x = x_ref[0]
    h = (_rms(x) * gpre_ref[...]) * (1.0 + mod_ref[0, 1:2, :]) + mod_ref[0, 0:1, :]
    br = jax.nn.sigmoid(_dot(h.astype(BF16), wbrg_ref[...]))

    u = u_ref[0]
    prev = jnp.where(i > 0, uprev_ref[0], 0.0)
    ext = jnp.concatenate([prev, u], axis=0)
    tpos = i * tm + lax.broadcasted_iota(jnp.int32, (tm, 1), 0)
    mixed = []
    for gidx, w in enumerate(POOL_SIZES):
        acc = ext[:, gidx * POOL_GROUP:(gidx + 1) * POOL_GROUP]
        span = 1
        while span < w:
            acc = acc + pltpu.roll(acc, span, 0)
            span *= 2
        cnt = jnp.minimum(tpos + 1, w).astype(F32)
        pooled = acc[POOL_HALO:] / cnt - u[:, gidx * POOL_GROUP:(gidx + 1) * POOL_GROUP]
        mixed.append(_dot(pooled.astype(BF16), wpool_ref[gidx]))
    o_pool = jnp.concatenate(mixed, axis=1) * pscale_ref[...]

    merged = (br[:, :D_MODEL] * _dot(on_ref[0], wbn_ref[...])
              + br[:, D_MODEL:] * _dot(o_pool.astype(BF16), wbp_ref[...]))
    y = _dot(merged.astype(BF16), wout_ref[...])
    o_ref[0] = x + mod_ref[0, 2:3, :] * (_rms(y) * gpost_ref[...])


def _merge(x, mod, g_pre, g_post, o_nsa, u, w_brg, w_pool, pool_scale, w_br_nsa, w_br_pool, w_out):
    b, s, d = x.shape
    tm = min(TM_PROJ, s)
    halo_per_tile = tm // POOL_HALO
    tile = lambda w: pl.BlockSpec((1, tm, w), lambda bi, i: (bi, i, 0))
    c2 = lambda shape: pl.BlockSpec(shape, lambda bi, i: (0, 0))
    return pl.pallas_call(
        _merge_kernel,
        grid=(b, s // tm),
        in_specs=[tile(d),
                  pl.BlockSpec((1, N_MOD, d), lambda bi, i: (bi, 0, 0)),
                  c2((1, d)), c2((1, d)),
                  tile(NSA_WIDTH), tile(POOL_WIDTH),
                  pl.BlockSpec((1, POOL_HALO, POOL_WIDTH),
                               lambda bi, i: (bi, jnp.maximum(i * halo_per_tile - 1, 0), 0)),
                  c2((d, 2 * d)),
                  pl.BlockSpec((len(POOL_SIZES), POOL_GROUP, POOL_GROUP), lambda bi, i: (0, 0, 0)),
                  c2((1, POOL_WIDTH)),
                  c2((NSA_WIDTH, d)), c2((POOL_WIDTH, d)), c2((d, d))],
        out_specs=tile(d),
        out_shape=jax.ShapeDtypeStruct((b, s, d), F32),
        compiler_params=_cparams(2),
        name="merge",
    )(x, mod, g_pre, g_post, o_nsa, u, u, w_brg, w_pool, pool_scale, w_br_nsa, w_br_pool, w_out)


def _mlp_kernel(x_ref, mod_ref, gpre_ref, gpost_ref, w1_ref, w2_ref, o_ref):
    x = x_ref[0]
    h = ((_rms(x) * gpre_ref[...]) * (1.0 + mod_ref[0, 4:5, :]) + mod_ref[0, 3:4, :]).astype(BF16)
    y = jnp.zeros(x.shape, F32)
    for c0 in range(0, D_FF, FF_CHUNK):
        a = jnp.maximum(_dot(h, w1_ref[:, c0:c0 + FF_CHUNK]), 0.0)
        y = y + _dot((a * a).astype(BF16), w2_ref[c0:c0 + FF_CHUNK, :])
    o_ref[0] = x + mod_ref[0, 5:6, :] * (_rms(y) * gpost_ref[...])


def _mlp(x, mod, g_pre, g_post, w_fc1, w_fc2):
    b, s, d = x.shape
    tm = min(TM_PROJ, s)
    tile = pl.BlockSpec((1, tm, d), lambda bi, i: (bi, i, 0))
    resident = lambda shape: pl.BlockSpec(shape, lambda bi, i: (0, 0), pipeline_mode=pl.Buffered(1))
    return pl.pallas_call(
        _mlp_kernel,
        grid=(b, s // tm),
        in_specs=[tile,
                  pl.BlockSpec((1, N_MOD, d), lambda bi, i: (bi, 0, 0)),
                  pl.BlockSpec((1, d), lambda bi, i: (0, 0)),
                  pl.BlockSpec((1, d), lambda bi, i: (0, 0)),
                  resident((d, D_FF)), resident((D_FF, d))],
        out_specs=tile,
        out_shape=jax.ShapeDtypeStruct((b, s, d), F32),
        compiler_params=_cparams(2),
        name="mlp",
    )(x, mod, g_pre, g_post, w_fc1, w_fc2)


def _rope_tables(s):
    half = HEAD_DIM // 2
    inv = ROPE_THETA ** (-jnp.arange(half, dtype=F32) / half)
    ang = jnp.arange(s, dtype=F32)[:, None] * inv[None, :]
    cos = jnp.cos(ang)
    sin = jnp.sin(ang)
    reps = LANES // HEAD_DIM
    cos_t = jnp.tile(jnp.concatenate([cos, cos], axis=1), (1, reps))
    sin_t = jnp.tile(jnp.concatenate([-sin, sin], axis=1), (1, reps))
    return cos_t, sin_t


def _overlap_table(nc, n_sel):
    cmp_start = np.arange(nc) * CMP_STRIDE
    sel_start = np.arange(n_sel) * SEL_BLOCK
    ovl = ((cmp_start[None, :] < sel_start[:, None] + SEL_BLOCK)
           & (cmp_start[None, :] + CMP_BLOCK > sel_start[:, None])).astype(np.float32)
    out = np.zeros((N_BLK, nc), np.float32)
    out[:n_sel] = ovl
    return jnp.asarray(out, dtype=BF16)


def _pack_in_weights(w_in):
    col_scale = jnp.concatenate([jnp.full((NSA_WIDTH,), HEAD_DIM ** -0.5 * LOG2E, F32),
                                 jnp.ones((w_in.shape[1] - NSA_WIDTH,), F32)])
    w = (w_in * col_scale).astype(BF16)
    q, kc, vc, ks, vs, kw, vw, gate, u, brg = jnp.split(w, IN_SPLITS, axis=-1)
    d = w_in.shape[0]
    gate = gate.reshape(d, N_KV, GQA_R * 3)
    gate = jnp.pad(gate, ((0, 0), (0, 0), (0, HEAD_DIM - GQA_R * 3))).reshape(d, KV_WIDTH)
    return jnp.concatenate([q, kc, ks, kw, vc, vs, vw, gate, u], axis=1), brg


def kernel(x, c, w_ada, b_ada, g_pre_mix, g_post_mix, w_in, cmp_pe_k, cmp_w1_k, cmp_w2_k,
           cmp_pe_v, cmp_w1_v, cmp_w2_v, w_pool, pool_scale, w_br_nsa, w_br_pool, w_out,
           g_pre_mlp, g_post_mlp, w_fc1, w_fc2):
    b, s, d = x.shape
    depth = w_ada.shape[0]
    n_sel = s // SEL_BLOCK
    nc = s // CMP_STRIDE
    assert d == D_MODEL and s % TM_PROJ == 0 and s >= WINDOW + TQ
    assert SEL_TOPK <= n_sel <= N_BLK
    cos_t, sin_t = _rope_tables(s)
    ovl = _overlap_table(nc, n_sel)

    for l in range(depth):
        mod = _modulation(c, w_ada, b_ada, l).reshape(b, N_MOD, d)
        w_cat, w_brg = _pack_in_weights(w_in[l])
        q, kc, vc, ks_aug, kw_aug, vs_p, vw_p, gl, u = _inproj(
            x, mod, g_pre_mix[l].reshape(1, d), w_cat, cos_t, sin_t)
        pek, w1k, w2k = _compress_weights(cmp_pe_k[l], cmp_w1_k[l], cmp_w2_k[l])
        pev, w1v, w2v = _compress_weights(cmp_pe_v[l], cmp_w1_v[l], cmp_w2_v[l])
        kc_aug, vc_p = _compress(kc, vc, pek, pev, w1k, w1v, w2k, w2v)
        o_nsa = _attention(q, kc_aug, vc_p, ks_aug, vs_p, kw_aug, vw_p, gl, ovl)
        x = _merge(x, mod, g_pre_mix[l].reshape(1, d), g_post_mix[l].reshape(1, d), o_nsa, u,
                   w_brg, w_pool[l].astype(BF16), pool_scale[l].reshape(1, POOL_WIDTH),
                   w_br_nsa[l].astype(BF16), w_br_pool[l].astype(BF16), w_out[l].astype(BF16))
        x = _mlp(x, mod, g_pre_mlp[l].reshape(1, d), g_post_mlp[l].reshape(1, d),
                 w_fc1[l].astype(BF16), w_fc2[l].astype(BF16))
    return x
```
